```python
import jax, jax.numpy as jnp
from jax import lax
import numpy as np

D_MODEL = 1024
BATCH = 8
SEQ = 2048
DEPTH = 4

MEM_LEN = 256
EPS = 1e-6
CONV_W = 4

MLSTM_HEADS = 4
MLSTM_DH = 128
MLSTM_W = MLSTM_HEADS * MLSTM_DH
MLSTM_CHUNK = 128
MLSTM_INIT_M = -1e30

LRU_W = 512
LRU_BLOCKS = 8
LRU_BD = LRU_W // LRU_BLOCKS
LRU_C = 8.0

SWA_HEADS = 8
SWA_KV_HEADS = 2
SWA_DH = 64
SWA_W = SWA_HEADS * SWA_DH
SWA_KV_W = SWA_KV_HEADS * SWA_DH
SWA_WINDOW = 128
SWA_BLOCK = 128

HGRN_HEADS = 4
HGRN_DK = 128
HGRN_DV = 128
HGRN_W = HGRN_HEADS * HGRN_DK
HGRN_CHUNK = 64

XA_HEADS = 4
XA_DH = 64
XA_W = XA_HEADS * XA_DH

BRANCH_W = (MLSTM_W, LRU_W, SWA_W, HGRN_HEADS * HGRN_DV, XA_W)
N_BRANCH = len(BRANCH_W)
MIX_W = sum(BRANCH_W)
IN_SIZES = (MLSTM_W, MLSTM_W, MLSTM_W, MLSTM_W, MLSTM_W, MLSTM_HEADS, MLSTM_HEADS,
            LRU_W, LRU_W,
            SWA_W, SWA_KV_W, SWA_KV_W, SWA_W,
            HGRN_W, HGRN_W, HGRN_HEADS * HGRN_DV, HGRN_HEADS * HGRN_DV,
            XA_W, XA_W,
            N_BRANCH * D_MODEL)
N_IN = sum(IN_SIZES)

kernel_name = "hybrid_gated_mlstm_rglru_swa_hgrn2"


def _split_points(sizes):
    pts, acc = [], 0
    for s in sizes[:-1]:
        acc += s
        pts.append(acc)
    return pts


def rmsnorm(x, g):
    xf = x.astype(jnp.float32)
    y = xf * lax.rsqrt(jnp.mean(xf * xf, axis=-1, keepdims=True) + EPS)
    return (y * g.astype(jnp.float32)).astype(x.dtype)


def head_rmsnorm(x, g, n_heads):
    B, S, W = x.shape
    xh = x.astype(jnp.float32).reshape(B, S, n_heads, W // n_heads)
    xh = xh * lax.rsqrt(jnp.mean(xh * xh, axis=-1, keepdims=True) + EPS)
    return xh.reshape(B, S, W) * g.astype(jnp.float32)


def causal_conv(x, w):
    K = w.shape[0]
    S = x.shape[1]
    xp = jnp.pad(x, ((0, 0), (K - 1, 0), (0, 0)))
    y = xp[:, 0:S] * w[0]
    for k in range(1, K):
        y = y + xp[:, k:k + S] * w[k]
    return y


def alibi_slopes(n):
    return 2.0 ** (-8.0 * jnp.arange(1, n + 1, dtype=jnp.float32) / n)


def mlstm_chunkwise(q, k, v, ig, fg):
    B, S, H, d = q.shape
    L = MLSTM_CHUNK
    nc = S // L
    f32 = jnp.float32

    def chunk(t):
        return t.reshape(B, nc, L, H, -1).transpose(0, 3, 1, 2, 4)

    qc, kc, vc = chunk(q * d ** -0.5), chunk(k), chunk(v)
    logf = jax.nn.log_sigmoid(fg).reshape(B, nc, L, H).transpose(0, 3, 1, 2)
    ic = ig.reshape(B, nc, L, H).transpose(0, 3, 1, 2)
    g = jnp.cumsum(logf, axis=-1)
    G = g[..., -1]
    causal = jnp.tril(jnp.ones((L, L), dtype=bool))
    Dm = jnp.where(causal, g[..., :, None] - g[..., None, :] + ic[..., None, :], -jnp.inf)
    w_end = G[..., None] - g + ic
    m_loc = jnp.max(w_end, axis=-1)
    e = jnp.exp(w_end - m_loc[..., None])
    C_loc = jnp.einsum('bhcl,bhcld,bhcle->bhcde', e, kc, vc)
    n_loc = jnp.einsum('bhcl,bhcld->bhcd', e, kc)

    def step(carry, inp):
        C, n, m = carry
        Gc, Cl, nl, ml = inp
        m_new = jnp.maximum(Gc + m, ml)
        a = jnp.exp(Gc + m - m_new)
        bco = jnp.exp(ml - m_new)
        C_new = a[..., None, None] * C + bco[..., None, None] * Cl
        n_new = a[..., None] * n + bco[..., None] * nl
        return (C_new, n_new, m_new), (C, n, m)

    init = (jnp.zeros((B, H, d, d), f32), jnp.zeros((B, H, d), f32), jnp.full((B, H), MLSTM_INIT_M, f32))
    xs = (G.transpose(2, 0, 1), C_loc.transpose(2, 0, 1, 3, 4), n_loc.transpose(2, 0, 1, 3), m_loc.transpose(2, 0, 1))
    _, (C_prev, n_prev, m_prev) = lax.scan(step, init, xs)
    C_prev = C_prev.transpose(1, 2, 0, 3, 4)
    n_prev = n_prev.transpose(1, 2, 0, 3)
    m_prev = m_prev.transpose(1, 2, 0)

    a_inter = g + m_prev[..., None]
    m_j = jnp.maximum(a_inter, jnp.max(Dm, axis=-1))
    P = jnp.exp(Dm - m_j[..., None]) * jnp.einsum('bhcjd,bhcrd->bhcjr', qc, kc)
    w_inter = jnp.exp(a_inter - m_j)
    num = jnp.einsum('bhcjr,bhcre->bhcje', P, vc) + w_inter[..., None] * jnp.einsum('bhcjd,bhcde->bhcje', qc, C_prev)
    den = jnp.sum(P, axis=-1) + w_inter * jnp.einsum('bhcjd,bhcd->bhcj', qc, n_prev)
    h = num / jnp.maximum(jnp.abs(den), jnp.exp(-m_j))[..., None]
    return h.transpose(0, 2, 3, 1, 4).reshape(B, S, H, d)


def rglru(x, conv_w, conv_b, wa, ba, wx, bx, lam):
    B, S, W = x.shape
    xc = causal_conv(x, conv_w) + conv_b
    xb = xc.reshape(B, S, LRU_BLOCKS, LRU_BD)
    r = jax.nn.sigmoid(jnp.einsum('bsnd,nde->bsne', xb, wa).reshape(B, S, W) + ba)
    i = jax.nn.sigmoid(jnp.einsum('bsnd,nde->bsne', xb, wx).reshape(B, S, W) + bx)
    log_a = -LRU_C * r * jax.nn.softplus(-lam)
    a = jnp.exp(log_a)
    u = jnp.sqrt(-jnp.expm1(2.0 * log_a)) * (i * xc)

    def combine(lhs, rhs):
        a1, b1 = lhs
        a2, b2 = rhs
        return a1 * a2, a2 * b1 + b2

    _, h = lax.associative_scan(combine, (a, u), axis=1)
    return h


def _with_prev_block(t):
    prev = jnp.pad(t, [(0, 0), (1, 0)] + [(0, 0)] * (t.ndim - 2))[:, :-1]
    return jnp.concatenate([prev, t], axis=2)


def swa_sink_alibi(q, k, v, pos, sinks):
    B, S, H, d = q.shape
    KV = k.shape[2]
    G = H // KV
    T = SWA_BLOCK
    nb = S // T
    qb = q.reshape(B, nb, T, KV, G, d)
    kb = _with_prev_block(k.reshape(B, nb, T, KV, d))
    vb = _with_prev_block(v.reshape(B, nb, T, KV, d))
    pq = pos.reshape(B, nb, T)
    pk = _with_prev_block(pq)
    s = jnp.einsum('bntkgd,bnskd->bnkgts', qb, kb) * d ** -0.5
    dist = jnp.abs(pq[:, :, :, None] - pk[:, :, None, :]).astype(jnp.float32)
    slopes = alibi_slopes(H).reshape(KV, G)
    s = s - slopes[None, None, :, :, None, None] * dist[:, :, None, None]
    ti = jnp.arange(T)[:, None] + T
    si = jnp.arange(2 * T)[None, :]
    rel = ti - si
    in_window = (rel >= 0) & (rel < SWA_WINDOW)
    blk = jnp.arange(nb)[:, None, None]
    valid = in_window[None] & ((si[None] >= T) | (blk > 0))
    s = jnp.where(valid[None, :, None, None], s, -jnp.inf)
    sink = jnp.broadcast_to(sinks.astype(jnp.float32).reshape(KV, G)[None, None, :, :, None, None], s.shape[:-1] + (1,))
    p = jax.nn.softmax(jnp.concatenate([s, sink], axis=-1), axis=-1)[..., :-1]
    o = jnp.einsum('bnkgts,bnskd->bntkgd', p, vb)
    return o.reshape(B, S, H * d)


def hgrn2_chunkwise(q, f_pre, i, lb):
    B, S, _ = q.shape
    H, dk, dv, L = HGRN_HEADS, HGRN_DK, HGRN_DV, HGRN_CHUNK
    nc = S // L
    f32 = jnp.float32
    lb = lb.astype(f32).reshape(H, dk)
    log_f = jnp.logaddexp(jnp.log(lb), jnp.log1p(-lb) + jax.nn.log_sigmoid(f_pre.reshape(B, S, H, dk)))
    k = -jnp.expm1(log_f)
    qs = jax.nn.silu(q.reshape(B, S, H, dk)) * dk ** -0.5

    def to_chunks(t):
        return t.reshape(B, nc, L, H, t.shape[-1]).transpose(1, 0, 3, 2, 4)

    xs = (to_chunks(qs), to_chunks(k), to_chunks(i.reshape(B, S, H, dv)), to_chunks(log_f))
    causal = jnp.tril(jnp.ones((L, L), dtype=bool))

    def step(state, inp):
        qc, kc, vc, gc = inp
        b = jnp.cumsum(gc, axis=2)
        diff = b[:, :, :, None, :] - b[:, :, None, :, :]
        decay = jnp.exp(jnp.where(causal[:, :, None], diff, -jnp.inf))
        a = jnp.einsum('bhjd,bhjrd,bhrd->bhjr', qc, decay, kc)
        o = jnp.einsum('bhjr,bhre->bhje', a, vc) + jnp.einsum('bhjd,bhde->bhje', qc * jnp.exp(b), state)
        b_end = b[:, :, -1:, :]
        state = jnp.exp(b_end[:, :, 0, :])[..., None] * state + jnp.einsum('bhrd,bhre->bhde', kc * jnp.exp(b_end - b), vc)
        return state, o

    s0 = jnp.zeros((B, H, dk, dv), f32)
    _, o = lax.scan(step, s0, xs)
    return o.transpose(1, 0, 3, 2, 4).reshape(B, S, H * dv)


def memory_cross_attention(q, km, vm):
    B, S, _ = q.shape
    M = km.shape[1]
    qh = q.reshape(B, S, XA_HEADS, XA_DH)
    kh = km.reshape(B, M, XA_HEADS, XA_DH)
    vh = vm.reshape(B, M, XA_HEADS, XA_DH)
    s = jnp.einsum('bshd,bmhd->bhsm', qh, kh) * XA_DH ** -0.5
    p = jax.nn.softmax(s.astype(jnp.float32), axis=-1)
    return jnp.einsum('bhsm,bmhd->bshd', p, vh).reshape(B, S, XA_W)


def setup_inputs(seed: int = 0) -> dict:
    key = jax.random.key(seed)
    ks = jax.random.split(key, 24)
    f32 = jnp.float32

    def nrm(k, shape, scale):
        return jax.random.normal(k, shape, f32) * scale

    x = nrm(ks[0], (BATCH, SEQ, D_MODEL), 1.0)
    mem = nrm(ks[1], (BATCH, MEM_LEN, D_MODEL), 1.0)
    positions = jax.random.randint(ks[2], (BATCH, 1), 0, 4096, jnp.int32) + jnp.arange(SEQ, dtype=jnp.int32)[None, :]
    norm_g = 1.0 + nrm(ks[3], (DEPTH, D_MODEL), 0.02)
    w_in = nrm(ks[4], (DEPTH, D_MODEL, N_IN), D_MODEL ** -0.5)
    mlstm_conv_w = nrm(ks[5], (DEPTH, CONV_W, 2 * MLSTM_W), CONV_W ** -0.5)
    f_bias = jnp.linspace(3.0, 6.0, MLSTM_HEADS, dtype=f32)
    mlstm_b_if = jnp.concatenate([nrm(ks[6], (DEPTH, MLSTM_HEADS), 0.1),
                                  f_bias[None, :] + nrm(ks[7], (DEPTH, MLSTM_HEADS), 0.1)], axis=-1)
    mlstm_norm_g = 1.0 + nrm(ks[8], (DEPTH, MLSTM_W), 0.02)
    lru_conv_w = nrm(ks[9], (DEPTH, CONV_W, LRU_W), CONV_W ** -0.5)
    lru_conv_b = nrm(ks[10], (DEPTH, LRU_W), 0.01)
    lru_wa = nrm(ks[11], (DEPTH, LRU_BLOCKS, LRU_BD, LRU_BD), LRU_BD ** -0.5)
    lru_ba = nrm(ks[12], (DEPTH, LRU_W), 0.01)
    lru_wx = nrm(ks[13], (DEPTH, LRU_BLOCKS, LRU_BD, LRU_BD), LRU_BD ** -0.5)
    lru_bx = nrm(ks[14], (DEPTH, LRU_W), 0.01)
    u = jax.random.uniform(ks[15], (DEPTH, LRU_W), f32, 0.9, 0.999)
    sig = u ** (1.0 / LRU_C)
    lru_lambda = jnp.log(sig) - jnp.log1p(-sig)
    swa_sinks = nrm(ks[16], (DEPTH, SWA_HEADS), 0.5)
    hgrn_lb = nrm(ks[17], (DEPTH, HGRN_W), 1.0)
    hgrn_norm_g = 1.0 + nrm(ks[18], (DEPTH, HGRN_HEADS * HGRN_DV), 0.02)
    mem_norm_g = 1.0 + nrm(ks[19], (DEPTH, D_MODEL), 0.02)
    w_mem_kv = nrm(ks[20], (DEPTH, D_MODEL, 2 * XA_W), D_MODEL ** -0.5)
    w_br = nrm(ks[21], (DEPTH, MIX_W, D_MODEL), MLSTM_W ** -0.5)
    w_out = nrm(ks[22], (DEPTH, D_MODEL, D_MODEL), D_MODEL ** -0.5)
    final_norm_g = 1.0 + nrm(ks[23], (D_MODEL,), 0.02)
    return {"x": x, "mem": mem, "positions": positions, "norm_g": norm_g, "w_in": w_in,
            "mlstm_conv_w": mlstm_conv_w, "mlstm_b_if": mlstm_b_if, "mlstm_norm_g": mlstm_norm_g,
            "lru_conv_w": lru_conv_w, "lru_conv_b": lru_conv_b, "lru_wa": lru_wa, "lru_ba": lru_ba,
            "lru_wx": lru_wx, "lru_bx": lru_bx, "lru_lambda": lru_lambda, "swa_sinks": swa_sinks,
            "hgrn_lb": hgrn_lb, "hgrn_norm_g": hgrn_norm_g, "mem_norm_g": mem_norm_g,
            "w_mem_kv": w_mem_kv, "w_br": w_br, "w_out": w_out, "final_norm_g": final_norm_g}


def reference(x, mem, positions, norm_g, w_in, mlstm_conv_w, mlstm_b_if, mlstm_norm_g,
              lru_conv_w, lru_conv_b, lru_wa, lru_ba, lru_wx, lru_bx, lru_lambda, swa_sinks,
              hgrn_lb, hgrn_norm_g, mem_norm_g, w_mem_kv, w_br, w_out, final_norm_g):
    f32 = jnp.float32
    B, S, _ = x.shape
    lb_all = jnp.cumsum(jax.nn.softmax(hgrn_lb.astype(f32), axis=0), axis=0)
    lb_all = lb_all - lb_all[0:1]
    in_pts = _split_points(IN_SIZES)
    br_pts = _split_points(BRANCH_W)
    for l in range(DEPTH):
        h = rmsnorm(x, norm_g[l])
        u = jnp.einsum('bsd,dn->bsn', h, w_in[l]).astype(f32)
        (qa, ka, va, oa, za, ia, fa, xb, zb, qc, kc, vc, zc,
         qd, fd, idd, zd, qe, ze, gates) = jnp.split(u, in_pts, axis=-1)

        qk = jax.nn.silu(causal_conv(jnp.concatenate([qa, ka], axis=-1), mlstm_conv_w[l].astype(f32)))
        qa2, ka2 = qk[..., :MLSTM_W], qk[..., MLSTM_W:]
        ig = ia + mlstm_b_if[l, :MLSTM_HEADS]
        fg = fa + mlstm_b_if[l, MLSTM_HEADS:]
        hm = mlstm_chunkwise(qa2.reshape(B, S, MLSTM_HEADS, MLSTM_DH), ka2.reshape(B, S, MLSTM_HEADS, MLSTM_DH),
                             va.reshape(B, S, MLSTM_HEADS, MLSTM_DH), ig, fg).reshape(B, S, MLSTM_W)
        ya = head_rmsnorm(hm, mlstm_norm_g[l], MLSTM_HEADS) * jax.nn.sigmoid(oa) * jax.nn.silu(za)

        yb = rglru(xb, lru_conv_w[l], lru_conv_b[l], lru_wa[l], lru_ba[l], lru_wx[l], lru_bx[l],
                   lru_lambda[l]) * jax.nn.silu(zb)

        yc = swa_sink_alibi(qc.reshape(B, S, SWA_HEADS, SWA_DH), kc.reshape(B, S, SWA_KV_HEADS, SWA_DH),
                            vc.reshape(B, S, SWA_KV_HEADS, SWA_DH), positions, swa_sinks[l]) * jax.nn.silu(zc)

        hd = hgrn2_chunkwise(qd, fd, idd, lb_all[l])
        yd = head_rmsnorm(hd, hgrn_norm_g[l], HGRN_HEADS) * jax.nn.silu(zd)

        memn = rmsnorm(mem, mem_norm_g[l])
        kvm = jnp.einsum('bmd,dn->bmn', memn, w_mem_kv[l]).astype(f32)
        ye = memory_cross_attention(qe, kvm[..., :XA_W], kvm[..., XA_W:]) * jax.nn.silu(ze)

        g = jax.nn.sigmoid(gates.reshape(B, S, N_BRANCH, D_MODEL))
        w_parts = jnp.split(w_br[l], br_pts, axis=0)
        ys = (ya, yb, yc, yd, ye)
        y = g[:, :, 0] * jnp.einsum('bsw,wd->bsd', ys[0], w_parts[0])
        for j in range(1, N_BRANCH):
            y = y + g[:, :, j] * jnp.einsum('bsw,wd->bsd', ys[j], w_parts[j])
        out = jnp.einsum('bsd,de->bse', y.astype(x.dtype), w_out[l])
        x = x + out.astype(x.dtype)
    return rmsnorm(x, final_norm_g)
```

```python
import functools

import jax
import jax.numpy as jnp
from jax import lax
from jax.experimental import pallas as pl
from jax.experimental.pallas import tpu as pltpu

F32 = jnp.float32
BF16 = jnp.bfloat16

EPS = 1e-6
CONV_W = 4
N_HEAD_A = 4
W_A = 512
L_A = 128
INIT_M = -1e30
W_B = 512
LRU_C = 8.0
N_HEAD_C = 8
W_C = 512
KV_W_C = 128
BLK_C = 128
N_HEAD_D = 4
W_D = 512
DIAG_D = 8
W_E = 256
N_BRANCH = 5
BRANCH_W = (W_A, W_B, W_C, W_D, W_E)

LANES = 128
SUBLANES = 8
VMEM_LIMIT_BYTES = 58 * 1024 * 1024

_OFF = {}
_acc = 0
for _name, _w in (("a_qk", 2 * W_A), ("a_v", W_A), ("a_o", W_A), ("a_z", W_A), ("a_if", LANES),
                  ("b_x", W_B), ("b_z", W_B),
                  ("c_q", W_C), ("c_k", KV_W_C), ("c_v", KV_W_C), ("c_z", W_C),
                  ("d_q", W_D), ("d_f", W_D), ("d_i", W_D), ("d_z", W_D),
                  ("e_q", W_E), ("e_z", W_E), ("gates", None)):
    _OFF[_name] = _acc
    if _w is not None:
        _acc += _w
N_IF = 2 * N_HEAD_A


def _bf(x):
    return x.astype(BF16)


def _dot(a, b):
    return jnp.dot(a, b, preferred_element_type=F32)


def _dot_nt(a, b):
    return lax.dot_general(a, b, (((1,), (1,)), ((), ())), preferred_element_type=F32)


def _dot_tn(a, b):
    return lax.dot_general(a, b, (((0,), (0,)), ((), ())), preferred_element_type=F32)


def _silu(x):
    return x * jax.nn.sigmoid(x)


def _log_sigmoid(x):
    return jnp.minimum(x, 0.0) - jnp.log1p(jnp.exp(-jnp.abs(x)))


def _softplus(x):
    return jnp.maximum(x, 0.0) + jnp.log1p(jnp.exp(-jnp.abs(x)))


def _rmsnorm(x, g):
    return x * lax.rsqrt(jnp.mean(x * x, axis=-1, keepdims=True) + EPS) * g


def _cumsum_rows(tri, x):
    hi = _bf(x)
    r1 = x - hi.astype(F32)
    mid = _bf(r1)
    lo = _bf(r1 - mid.astype(F32))
    return _dot(tri, hi) + _dot(tri, mid) + _dot(tri, lo)


def _block_row_bcast(b, blk, idx):
    n = b.shape[0] // blk
    b3 = b.reshape(n, blk, b.shape[1])
    return jnp.broadcast_to(b3[:, idx:idx + 1, :], b3.shape).reshape(b.shape)


def _causal_conv(buf, x, w_ref, t):
    buf[pl.ds(SUBLANES, t), :] = x
    y = w_ref[0:1, :] * buf[pl.ds(SUBLANES - 3, t), :]
    for k in range(1, CONV_W):
        y = y + w_ref[k:k + 1, :] * buf[pl.ds(SUBLANES - 3 + k, t), :]
    buf[pl.ds(0, SUBLANES), :] = buf[pl.ds(t, SUBLANES), :]
    return y


def _mlstm_head(q, k, v, g, icol, grow, irow, c_st, n_st, m_st, causal):
    L = q.shape[0]
    dm = jnp.where(causal, g - grow + irow, -jnp.inf)
    g_end = g[L - 1:L, :]
    w_end = g_end - g + icol
    m_loc = jnp.max(w_end, axis=0, keepdims=True)
    ke = k * jnp.exp(w_end - m_loc)
    vb = _bf(v)
    c_loc = _dot_tn(_bf(ke), vb)
    n_loc = jnp.sum(ke, axis=0, keepdims=True)

    a_inter = g + m_st
    m_j = jnp.maximum(a_inter, jnp.max(dm, axis=-1, keepdims=True))
    qb = _bf(q)
    p = jnp.exp(dm - m_j) * _dot_nt(qb, _bf(k))
    w_inter = jnp.exp(a_inter - m_j)
    num = _dot(_bf(p), vb) + w_inter * _dot(qb, _bf(c_st))
    den = jnp.sum(p, axis=-1, keepdims=True) + w_inter * jnp.sum(q * n_st, axis=-1, keepdims=True)
    h = num / jnp.maximum(jnp.abs(den), jnp.exp(-m_j))

    m_new = jnp.maximum(g_end + m_st, m_loc)
    a = jnp.exp(g_end + m_st - m_new)
    b = jnp.exp(m_loc - m_new)
    return h, a * c_st + b * c_loc, a * n_st + b * n_loc, m_new


def _hgrn_head(qs, kk, v, b, st_t, ri, ci):
    t = qs.shape[0]
    rows = ri[:, 0:1]
    o = _dot_nt(_bf(qs * jnp.exp(b)), _bf(st_t))
    b_end = b[t - 1:t, :]
    vb = _bf(v)
    st_new = st_t * jnp.exp(b_end) + _dot_tn(vb, _bf(kk * jnp.exp(b_end - b)))

    amat = None
    s = t // 2
    while s >= DIAG_D:
        sh = (2 * s).bit_length() - 1
        d = b - _block_row_bcast(b, 2 * s, s - 1)
        upper = (rows & (2 * s - 1)) >= s
        ql = qs * jnp.where(upper, jnp.exp(d), 0.0)
        kl = kk * jnp.where(upper, 0.0, jnp.exp(-d))
        al = _dot_nt(_bf(ql), _bf(kl))
        same = (ri >> sh) == (ci >> sh)
        amat = jnp.where(same, al, 0.0 if amat is None else amat)
        s //= 2
    sh = DIAG_D.bit_length() - 1
    d = b - _block_row_bcast(b, DIAG_D, 0)
    ad = _dot_nt(_bf(qs * jnp.exp(d)), _bf(kk * jnp.exp(-d)))
    diag = ((ri >> sh) == (ci >> sh)) & (ci <= ri)
    amat = jnp.where(diag, ad, amat)
    return o + _dot(_bf(amat), vb), st_new


def _pair_attention(qp, kmat, vmat, scale, bias_fn, sink_col):
    t = qp.shape[0]
    lane = lax.broadcasted_iota(jnp.int32, qp.shape, 1)
    lo = lane < (LANES // 2)
    q2 = jnp.concatenate([jnp.where(lo, qp, 0.0), jnp.where(lo, 0.0, qp)], axis=0)
    s = _dot_nt(_bf(q2), kmat) * scale
    s = bias_fn(s)
    m = jnp.max(s, axis=-1, keepdims=True)
    if sink_col is not None:
        m = jnp.maximum(m, sink_col)
    p = jnp.exp(s - m)
    den = jnp.sum(p, axis=-1, keepdims=True)
    if sink_col is not None:
        den = den + jnp.exp(sink_col - m)
    o2 = _dot(_bf(p / den), vmat)
    return jnp.where(lo, o2[:t], o2[t:])


def _layer_kernel(x_ref, pc_ref, pr_ref, mem_ref, ng_ref, win_ref, cwa_ref, bif_ref, mng_ref,
                  lcw_ref, lcb_ref, wlru_ref, lba_ref, lbx_ref, lam_ref, sink_ref, hlb_ref, hng_ref,
                  memg_ref, wkv_ref, wbr_ref, wout_ref, fg_ref,
                  o_ref,
                  conv_a, conv_b, m_c, m_n, m_m, lru_h, swa_k, swa_v, swa_p, h_st, xa_k, xa_v,
                  *, layer, final, tq):
    c = pl.program_id(1)
    d_model = x_ref.shape[-1]

    @pl.when(c == 0)
    def _init():
        conv_a[pl.ds(0, SUBLANES), :] = jnp.zeros((SUBLANES, conv_a.shape[1]), F32)
        conv_b[pl.ds(0, SUBLANES), :] = jnp.zeros((SUBLANES, conv_b.shape[1]), F32)
        m_c[...] = jnp.zeros(m_c.shape, F32)
        m_n[...] = jnp.zeros(m_n.shape, F32)
        m_m[...] = jnp.full(m_m.shape, INIT_M, F32)
        lru_h[...] = jnp.zeros(lru_h.shape, F32)
        swa_k[...] = jnp.zeros(swa_k.shape, BF16)
        swa_v[...] = jnp.zeros(swa_v.shape, BF16)
        swa_p[...] = jnp.zeros(swa_p.shape, jnp.int32)
        h_st[...] = jnp.zeros(h_st.shape, F32)
        memn = _bf(_rmsnorm(mem_ref[0], memg_ref[...]))
        kv = _dot(memn, wkv_ref[...])
        xa_k[...] = _bf(kv[:, :W_E])
        xa_v[...] = _bf(kv[:, W_E:])

    x = x_ref[0]
    hb = _bf(_rmsnorm(x, ng_ref[...]))

    def proj(name, width):
        o = _OFF[name]
        return _dot(hb, win_ref[:, o:o + width])

    rows_t = lax.broadcasted_iota(jnp.int32, (tq, 1), 0)

    qk = _silu(_causal_conv(conv_a, proj("a_qk", 2 * W_A), cwa_ref, tq))
    v_a = proj("a_v", W_A)
    gif = proj("a_if", LANES) + bif_ref[...]
    logf = _log_sigmoid(gif)
    ri_a = lax.broadcasted_iota(jnp.int32, (L_A, L_A), 0)
    ci_a = lax.broadcasted_iota(jnp.int32, (L_A, L_A), 1)
    causal_a = ci_a <= ri_a
    tri_a = jnp.where(causal_a, 1.0, 0.0).astype(BF16)
    hm_chunks = []
    for n in range(tq // L_A):
        rs = slice(n * L_A, (n + 1) * L_A)
        gif_c = gif[rs]
        g_all = _cumsum_rows(tri_a, logf[rs])
        g_t = g_all.T
        gif_t = gif_c.T
        heads = []
        for h in range(N_HEAD_A):
            ls = slice(h * LANES, (h + 1) * LANES)
            hh, c_new, n_new, m_new = _mlstm_head(
                qk[rs, ls] * (LANES ** -0.5), qk[rs, W_A + h * LANES:W_A + (h + 1) * LANES], v_a[rs, ls],
                g_all[:, N_HEAD_A + h:N_HEAD_A + h + 1], gif_c[:, h:h + 1],
                g_t[N_HEAD_A + h:N_HEAD_A + h + 1, :], gif_t[h:h + 1, :],
                m_c[h], m_n[h, 0:1, :], m_m[h, 0:1, 0:1], causal_a)
            m_c[h] = c_new
            m_n[h] = jnp.broadcast_to(n_new, m_n.shape[1:])
            m_m[h] = jnp.broadcast_to(m_new, m_m.shape[1:])
            hh = hh * lax.rsqrt(jnp.mean(hh * hh, axis=-1, keepdims=True) + EPS)
            heads.append(hh)
        hm_chunks.append(jnp.concatenate(heads, axis=1))
    hm = jnp.concatenate(hm_chunks, axis=0) if len(hm_chunks) > 1 else hm_chunks[0]
    ya = hm * mng_ref[...] * jax.nn.sigmoid(proj("a_o", W_A)) * _silu(proj("a_z", W_A))

    xc = _causal_conv(conv_b, proj("b_x", W_B), lcw_ref, tq) + lcb_ref[...]
    xcb = _bf(xc)
    half = W_B // 2
    rx0 = _dot(xcb[:, :half], wlru_ref[0])
    rx1 = _dot(xcb[:, half:], wlru_ref[1])
    r = jax.nn.sigmoid(jnp.concatenate([rx0[:, :half], rx1[:, :half]], axis=1) + lba_ref[...])
    i_g = jax.nn.sigmoid(jnp.concatenate([rx0[:, half:], rx1[:, half:]], axis=1) + lbx_ref[...])
    a_t = jnp.exp(-LRU_C * r * _softplus(-lam_ref[...]))
    b_t = jnp.sqrt(1.0 - a_t * a_t) * (i_g * xc)
    s = 1
    while s < tq:
        keep = rows_t >= s
        a_s = jnp.where(keep, pltpu.roll(a_t, s, 0), 1.0)
        b_s = jnp.where(keep, pltpu.roll(b_t, s, 0), 0.0)
        b_t = a_t * b_s + b_t
        a_t = a_t * a_s
        s *= 2
    h_lru = b_t + a_t * lru_h[0:1, :]
    lru_h[...] = jnp.broadcast_to(h_lru[tq - 1:tq, :], lru_h.shape)
    yb = h_lru * _silu(proj("b_z", W_B))

    q_c = proj("c_q", W_C)
    k_c = _bf(proj("c_k", KV_W_C))
    v_c = _bf(proj("c_v", KV_W_C))
    pos_c = pc_ref[0]
    pos_r = pr_ref[0]
    hd = LANES // 2
    rows2 = lax.broadcasted_iota(jnp.int32, (2 * BLK_C, 1), 0)
    ri_c = lax.broadcasted_iota(jnp.int32, (BLK_C, 2 * BLK_C), 0)
    ci_c = lax.broadcasted_iota(jnp.int32, (BLK_C, 2 * BLK_C), 1)
    rel = ri_c + BLK_C - ci_c
    in_window = (rel >= 0) & (rel < BLK_C)
    yc_blocks = []
    for n in range(tq // BLK_C):
        rs = slice(n * BLK_C, (n + 1) * BLK_C)
        k_prev = swa_k[...] if n == 0 else k_c[(n - 1) * BLK_C:n * BLK_C]
        v_prev = swa_v[...] if n == 0 else v_c[(n - 1) * BLK_C:n * BLK_C]
        p_prev = swa_p[0:1, :] if n == 0 else pos_r[:, (n - 1) * BLK_C:n * BLK_C]
        kk = jnp.concatenate([k_prev, k_c[rs]], axis=0)
        vv = jnp.concatenate([v_prev, v_c[rs]], axis=0)
        pk = jnp.concatenate([p_prev, pos_r[:, rs]], axis=1)
        dist = jnp.abs(pos_c[rs] - pk).astype(F32)
        first_key = jnp.where((c * (tq // BLK_C) + n) == 0, BLK_C, 0)
        valid = in_window & (ci_c >= first_key)
        dist2 = jnp.concatenate([dist, dist], axis=0)
        valid2 = jnp.concatenate([valid, valid], axis=0)
        kdup = [jnp.concatenate([kk[:, j * hd:(j + 1) * hd]] * 2, axis=1) for j in range(2)]
        vdup = [jnp.concatenate([vv[:, j * hd:(j + 1) * hd]] * 2, axis=1) for j in range(2)]
        pairs = []
        for p in range(N_HEAD_C // 2):
            kvh = (2 * p) // (N_HEAD_C // 2)
            top = rows2 < BLK_C
            slope = jnp.where(top, 2.0 ** (-(2 * p + 1)), 2.0 ** (-(2 * p + 2)))
            sink = jnp.where(top, sink_ref[0, 2 * p], sink_ref[0, 2 * p + 1])

            def bias(sc, slope=slope, dist2=dist2, valid2=valid2):
                return jnp.where(valid2, sc - slope * dist2, -jnp.inf)

            pairs.append(_pair_attention(q_c[rs, p * LANES:(p + 1) * LANES], kdup[kvh], vdup[kvh],
                                         hd ** -0.5, bias, sink))
        yc_blocks.append(jnp.concatenate(pairs, axis=1))
    last = slice(tq - BLK_C, tq)
    swa_k[...] = k_c[last]
    swa_v[...] = v_c[last]
    swa_p[...] = jnp.broadcast_to(pos_r[:, last], swa_p.shape)
    yc = (jnp.concatenate(yc_blocks, axis=0) if len(yc_blocks) > 1 else yc_blocks[0]) * _silu(proj("c_z", W_C))

    lbp = hlb_ref[...]
    e_lb = jnp.exp(lbp - jnp.max(lbp, axis=0, keepdims=True))
    p_lb = e_lb / jnp.sum(e_lb, axis=0, keepdims=True)
    lb = jnp.zeros((1, W_D), F32)
    for j in range(1, layer + 1):
        lb = lb + p_lb[j:j + 1, :]
    ls_f = jnp.log1p(-lb) + _log_sigmoid(proj("d_f", W_D))
    log_lb = jnp.log(lb)
    mx = jnp.maximum(log_lb, ls_f)
    log_f = mx + jnp.log1p(jnp.exp(-jnp.abs(log_lb - ls_f)))
    k_d = 1.0 - jnp.exp(log_f)
    qs_d = _silu(proj("d_q", W_D)) * (LANES ** -0.5)
    v_d = proj("d_i", W_D)
    ri_d = lax.broadcasted_iota(jnp.int32, (tq, tq), 0)
    ci_d = lax.broadcasted_iota(jnp.int32, (tq, tq), 1)
    tri_d = jnp.where(ci_d <= ri_d, 1.0, 0.0).astype(BF16)
    b_all = _cumsum_rows(tri_d, log_f)
    hd_heads = []
    for h in range(N_HEAD_D):
        ls = slice(h * LANES, (h + 1) * LANES)
        o_h, st_new = _hgrn_head(qs_d[:, ls], k_d[:, ls], v_d[:, ls], b_all[:, ls], h_st[h], ri_d, ci_d)
        h_st[h] = st_new
        hd_heads.append(o_h * lax.rsqrt(jnp.mean(o_h * o_h, axis=-1, keepdims=True) + EPS))
    yd = jnp.concatenate(hd_heads, axis=1) * hng_ref[...] * _silu(proj("d_z", W_D))

    q_e = proj("e_q", W_E)
    ye = jnp.concatenate(
        [_pair_attention(q_e[:, p * LANES:(p + 1) * LANES], xa_k[:, p * LANES:(p + 1) * LANES],
                         xa_v[:, p * LANES:(p + 1) * LANES], hd ** -0.5, lambda sc: sc, None)
         for p in range(W_E // LANES)], axis=1) * _silu(proj("e_z", W_E))

    y = None
    row0 = 0
    for j, (yj, wj) in enumerate(zip((ya, yb, yc, yd, ye), BRANCH_W)):
        g0 = _OFF["gates"] + j * d_model
        gj = jax.nn.sigmoid(_dot(hb, win_ref[:, g0:g0 + d_model]))
        t_j = gj * _dot(_bf(yj), wbr_ref[row0:row0 + wj, :])
        y = t_j if y is None else y + t_j
        row0 += wj
    x_new = x + _dot(_bf(y), wout_ref[...])
    if final:
        x_new = _rmsnorm(x_new, fg_ref[...])
    o_ref[0] = x_new


def _const_spec(shape):
    nd = len(shape)
    return pl.BlockSpec(shape, lambda b, c: (0,) * nd, pipeline_mode=pl.Buffered(1))


def _layer_call(x, pos_col, pos_row, mem, params, *, layer, final, tq):
    bsz, seq, d_model = x.shape
    m_len = mem.shape[1]
    in_specs = [
        pl.BlockSpec((1, tq, d_model), lambda b, c: (b, c, 0)),
        pl.BlockSpec((1, tq, 1), lambda b, c: (b, c, 0)),
        pl.BlockSpec((1, 1, tq), lambda b, c: (b, 0, c)),
        pl.BlockSpec((1, m_len, d_model), lambda b, c: (b, 0, 0)),
    ]
    for name, p in params:
        if name == "sinks":
            in_specs.append(pl.BlockSpec(memory_space=pltpu.SMEM))
        else:
            in_specs.append(_const_spec(p.shape))
    scratch = [
        pltpu.VMEM((tq + SUBLANES, 2 * W_A), F32),
        pltpu.VMEM((tq + SUBLANES, W_B), F32),
        pltpu.VMEM((N_HEAD_A, LANES, LANES), F32),
        pltpu.VMEM((N_HEAD_A, SUBLANES, LANES), F32),
        pltpu.VMEM((N_HEAD_A, SUBLANES, LANES), F32),
        pltpu.VMEM((SUBLANES, W_B), F32),
        pltpu.VMEM((BLK_C, KV_W_C), BF16),
        pltpu.VMEM((BLK_C, KV_W_C), BF16),
        pltpu.VMEM((SUBLANES, BLK_C), jnp.int32),
        pltpu.VMEM((N_HEAD_D, LANES, LANES), F32),
        pltpu.VMEM((m_len, W_E), BF16),
        pltpu.VMEM((m_len, W_E), BF16),
    ]
    return pl.pallas_call(
        functools.partial(_layer_kernel, layer=layer, final=final, tq=tq),
        grid=(bsz, seq // tq),
        in_specs=in_specs,
        out_specs=pl.BlockSpec((1, tq, d_model), lambda b, c: (b, c, 0)),
        out_shape=jax.ShapeDtypeStruct(x.shape, x.dtype),
        scratch_shapes=scratch,
        compiler_params=pltpu.CompilerParams(dimension_semantics=("arbitrary", "arbitrary"),
                                             vmem_limit_bytes=VMEM_LIMIT_BYTES),
        name=f"hybrid_layer_{layer}",
    )(x, pos_col, pos_row, mem, *[p for _, p in params])


def _tile_rows(seq):
    for t in (256, 128):
        if seq % t == 0:
            return t
    raise ValueError("sequence length must be a multiple of 128")


def kernel(x, mem, positions, norm_g, w_in, mlstm_conv_w, mlstm_b_if, mlstm_norm_g, lru_conv_w, lru_conv_b, lru_wa, lru_ba, lru_wx, lru_bx, lru_lambda, swa_sinks, hgrn_lb, hgrn_norm_g, mem_norm_g, w_mem_kv, w_br, w_out, final_norm_g):
    depth, d_model, n_in = w_in.shape
    n_a = 5 * W_A
    assert n_in == n_a + N_IF + 2 * W_B + 2 * W_C + 2 * KV_W_C + 4 * W_D + 2 * W_E + N_BRANCH * d_model
    tq = _tile_rows(x.shape[1])

    w_in_p = jnp.concatenate([w_in[:, :, :n_a + N_IF], jnp.zeros((depth, d_model, LANES - N_IF), w_in.dtype),
                              w_in[:, :, n_a + N_IF:]], axis=-1).astype(BF16)
    b_if = jnp.pad(mlstm_b_if, ((0, 0), (0, LANES - N_IF)))[:, None, :]
    nb, bd = lru_wa.shape[1], lru_wa.shape[2]
    eye = jnp.eye(nb, dtype=lru_wa.dtype)

    def block_diag(w):
        return (eye[None, :, None, :, None] * w[:, :, :, None, :]).reshape(depth, nb * bd, nb * bd)

    wa_d, wx_d = block_diag(lru_wa), block_diag(lru_wx)
    hw = W_B // 2
    w_lru = jnp.stack([jnp.concatenate([wa_d[:, :hw, :hw], wx_d[:, :hw, :hw]], axis=-1),
                       jnp.concatenate([wa_d[:, hw:, hw:], wx_d[:, hw:, hw:]], axis=-1)], axis=1).astype(BF16)
    w_kv = w_mem_kv.astype(BF16)
    w_br_b = w_br.astype(BF16)
    w_out_b = w_out.astype(BF16)

    pos_col = positions[:, :, None]
    pos_row = positions[:, None, :]
    row = lambda a, l: a[l][None, :]
    for l in range(depth):
        params = [
            ("norm_g", row(norm_g, l)), ("w_in", w_in_p[l]), ("conv_a", mlstm_conv_w[l]), ("b_if", b_if[l]),
            ("mlstm_norm_g", row(mlstm_norm_g, l)), ("lru_conv_w", lru_conv_w[l]), ("lru_conv_b", row(lru_conv_b, l)),
            ("w_lru", w_lru[l]), ("lru_ba", row(lru_ba, l)), ("lru_bx", row(lru_bx, l)), ("lru_lambda", row(lru_lambda, l)),
            ("sinks", row(swa_sinks, l)), ("hgrn_lb", hgrn_lb), ("hgrn_norm_g", row(hgrn_norm_g, l)),
            ("mem_norm_g", row(mem_norm_g, l)), ("w_kv", w_kv[l]), ("w_br", w_br_b[l]), ("w_out", w_out_b[l]),
            ("final_norm_g", final_norm_g[None, :]),
        ]
        x = _layer_call(x, pos_col, pos_row, mem, params, layer=l, final=(l == depth - 1), tq=tq)
    return x
```

```python
import functools

import jax
import jax.numpy as jnp
from jax import lax
from jax.experimental import pallas as pl
from jax.experimental.pallas import tpu as pltpu

F32 = jnp.float32
BF16 = jnp.bfloat16

EPS = 1e-6
CONV_W = 4
N_HEAD_A = 4
W_A = 512
L_A = 128
INIT_M = -1e30
W_B = 512
LRU_C = 8.0
N_HEAD_C = 8
W_C = 512
KV_W_C = 128
BLK_C = 128
N_HEAD_D = 4
W_D = 512
DIAG_D = 8
W_E = 256
N_BRANCH = 5
BRANCH_W = (W_A, W_B, W_C, W_D, W_E)

LANES = 128
SUBLANES = 8
VMEM_LIMIT_BYTES = 58 * 1024 * 1024

_OFF = {}
_acc = 0
for _name, _w in (("a_qk", 2 * W_A), ("a_v", W_A), ("a_o", W_A), ("a_z", W_A), ("a_if", LANES),
                  ("b_x", W_B), ("b_z", W_B),
                  ("c_q", W_C), ("c_k", KV_W_C), ("c_v", KV_W_C), ("c_z", W_C),
                  ("d_q", W_D), ("d_f", W_D), ("d_i", W_D), ("d_z", W_D),
                  ("e_q", W_E), ("e_z", W_E), ("gates", None)):
    _OFF[_name] = _acc
    if _w is not None:
        _acc += _w
N_IF = 2 * N_HEAD_A


def _bf(x):
    return x.astype(BF16)


def _dot(a, b):
    return jnp.dot(a, b, preferred_element_type=F32)


def _dot_nt(a, b):
    return lax.dot_general(a, b, (((1,), (1,)), ((), ())), preferred_element_type=F32)


def _dot_tn(a, b):
    return lax.dot_general(a, b, (((0,), (0,)), ((), ())), preferred_element_type=F32)


def _silu(x):
    return x * jax.nn.sigmoid(x)


def _log_sigmoid(x):
    return jnp.minimum(x, 0.0) - jnp.log(1.0 + jnp.exp(-jnp.abs(x)))


def _softplus(x):
    return jnp.maximum(x, 0.0) + jnp.log(1.0 + jnp.exp(-jnp.abs(x)))


def _linear_scan_rows(a, b, h0):
    t, w = a.shape
    groups = t // SUBLANES
    a3 = a.reshape(groups, SUBLANES, w)
    b3 = b.reshape(groups, SUBLANES, w)
    sub = lax.broadcasted_iota(jnp.int32, a3.shape, 1)
    s = 1
    while s < SUBLANES:
        keep = sub >= s
        a_s = jnp.where(keep, pltpu.roll(a3, s, 1), 1.0)
        b_s = jnp.where(keep, pltpu.roll(b3, s, 1), 0.0)
        b3 = a3 * b_s + b3
        a3 = a3 * a_s
        s *= 2
    carry = h0
    out = []
    for g in range(groups):
        hg = b3[g] + a3[g] * carry
        out.append(hg)
        carry = hg[SUBLANES - 1:SUBLANES, :]
    return jnp.concatenate(out, axis=0)


def _rmsnorm(x, g):
    return x * lax.rsqrt(jnp.mean(x * x, axis=-1, keepdims=True) + EPS) * g


def _cumsum_rows(tri, x):
    hi = _bf(x)
    r1 = x - hi.astype(F32)
    mid = _bf(r1)
    lo = _bf(r1 - mid.astype(F32))
    return _dot(tri, hi) + _dot(tri, mid) + _dot(tri, lo)


def _block_row_bcast(b, blk, idx):
    n = b.shape[0] // blk
    b3 = b.reshape(n, blk, b.shape[1])
    return jnp.broadcast_to(b3[:, idx:idx + 1, :], b3.shape).reshape(b.shape)


def _causal_conv(buf, x, w_ref, t):
    buf[pl.ds(SUBLANES, t), :] = x
    y = w_ref[0:1, :] * buf[pl.ds(SUBLANES - 3, t), :]
    for k in range(1, CONV_W):
        y = y + w_ref[k:k + 1, :] * buf[pl.ds(SUBLANES - 3 + k, t), :]
    buf[pl.ds(0, SUBLANES), :] = buf[pl.ds(t, SUBLANES), :]
    return y


def _mlstm_head(q, k, v, g, icol, grow, irow, c_st, n_st, m_st, causal):
    L = q.shape[0]
    dm = jnp.where(causal, g - grow + irow, -jnp.inf)
    g_end = g[L - 1:L, :]
    w_end = g_end - g + icol
    m_loc = jnp.max(w_end, axis=0, keepdims=True)
    ke = k * jnp.exp(w_end - m_loc)
    vb = _bf(v)
    c_loc = _dot_tn(_bf(ke), vb)
    n_loc = jnp.sum(ke, axis=0, keepdims=True)

    a_inter = g + m_st
    m_j = jnp.maximum(a_inter, jnp.max(dm, axis=-1, keepdims=True))
    qb = _bf(q)
    p = jnp.exp(dm - m_j) * _dot_nt(qb, _bf(k))
    w_inter = jnp.exp(a_inter - m_j)
    num = _dot(_bf(p), vb) + w_inter * _dot(qb, _bf(c_st))
    den = jnp.sum(p, axis=-1, keepdims=True) + w_inter * jnp.sum(q * n_st, axis=-1, keepdims=True)
    h = num / jnp.maximum(jnp.abs(den), jnp.exp(-m_j))

    m_new = jnp.maximum(g_end + m_st, m_loc)
    a = jnp.exp(g_end + m_st - m_new)
    b = jnp.exp(m_loc - m_new)
    return h, a * c_st + b * c_loc, a * n_st + b * n_loc, m_new


def _hgrn_head(qs, kk, v, b, st_t, ri, ci):
    t = qs.shape[0]
    rows = ri[:, 0:1]
    o = _dot_nt(_bf(qs * jnp.exp(b)), _bf(st_t))
    b_end = b[t - 1:t, :]
    vb = _bf(v)
    st_new = st_t * jnp.exp(b_end) + _dot_tn(vb, _bf(kk * jnp.exp(b_end - b)))

    amat = None
    s = t // 2
    while s >= DIAG_D:
        sh = (2 * s).bit_length() - 1
        e = jnp.exp(-jnp.abs(b - _block_row_bcast(b, 2 * s, s - 1)))
        upper = (rows & (2 * s - 1)) >= s
        ql = qs * jnp.where(upper, e, 0.0)
        kl = kk * jnp.where(upper, 0.0, e)
        al = _dot_nt(_bf(ql), _bf(kl))
        same = (ri >> sh) == (ci >> sh)
        amat = jnp.where(same, al, 0.0 if amat is None else amat)
        s //= 2
    sh = DIAG_D.bit_length() - 1
    d = b - _block_row_bcast(b, DIAG_D, 0)
    ad = _dot_nt(_bf(qs * jnp.exp(d)), _bf(kk * jnp.exp(-d)))
    diag = ((ri >> sh) == (ci >> sh)) & (ci <= ri)
    amat = jnp.where(diag, ad, amat)
    return o + _dot(_bf(amat), vb), st_new


def _pair_attention(qp, kmat, vmat, scale, bias_fn, sink_col):
    t = qp.shape[0]
    lane = lax.broadcasted_iota(jnp.int32, qp.shape, 1)
    lo = lane < (LANES // 2)
    q2 = jnp.concatenate([jnp.where(lo, qp, 0.0), jnp.where(lo, 0.0, qp)], axis=0)
    s = _dot_nt(_bf(q2), kmat) * scale
    s = bias_fn(s)
    m = jnp.max(s, axis=-1, keepdims=True)
    if sink_col is not None:
        m = jnp.maximum(m, sink_col)
    p = jnp.exp(s - m)
    den = jnp.sum(p, axis=-1, keepdims=True)
    if sink_col is not None:
        den = den + jnp.exp(sink_col - m)
    o2 = _dot(_bf(p / den), vmat)
    return jnp.where(lo, o2[:t], o2[t:])


def _mem_kv_kernel(mem_ref, g_ref, w_ref, o_ref):
    o_ref[...] = _bf(_dot(_bf(_rmsnorm(mem_ref[0], g_ref[...])), w_ref[...]))


def _mem_kv_call(mem, mem_norm_g, w_kv):
    bsz, m_len, d_model = mem.shape
    depth, _, n_kv = w_kv.shape
    return pl.pallas_call(
        _mem_kv_kernel,
        grid=(depth, bsz),
        in_specs=[pl.BlockSpec((1, m_len, d_model), lambda l, b: (b, 0, 0)),
                  pl.BlockSpec((None, 1, d_model), lambda l, b: (l, 0, 0)),
                  pl.BlockSpec((None, d_model, n_kv), lambda l, b: (l, 0, 0))],
        out_specs=pl.BlockSpec((None, None, m_len, n_kv), lambda l, b: (l, b, 0, 0)),
        out_shape=jax.ShapeDtypeStruct((depth, bsz, m_len, n_kv), BF16),
        compiler_params=pltpu.CompilerParams(dimension_semantics=("arbitrary", "arbitrary")),
        name="mem_kv",
    )(mem, mem_norm_g[:, None, :], w_kv)


def _layer_kernel(x_ref, pc_ref, pr_ref, kv_ref, ng_ref, win_ref, cwa_ref, bif_ref, mng_ref,
                  lcw_ref, lcb_ref, wlru_ref, lba_ref, lbx_ref, lam_ref, sink_ref, hlb_ref, hng_ref,
                  wbr_ref, wout_ref, fg_ref,
                  o_ref,
                  conv_a, conv_b, m_c, m_n, m_m, lru_h, swa_k, swa_v, swa_p, h_st,
                  *, layer, final, tq):
    c = pl.program_id(1)
    d_model = x_ref.shape[-1]

    @pl.when(c == 0)
    def _init():
        conv_a[pl.ds(0, SUBLANES), :] = jnp.zeros((SUBLANES, conv_a.shape[1]), F32)
        conv_b[pl.ds(0, SUBLANES), :] = jnp.zeros((SUBLANES, conv_b.shape[1]), F32)
        m_c[...] = jnp.zeros(m_c.shape, F32)
        m_n[...] = jnp.zeros(m_n.shape, F32)
        m_m[...] = jnp.full(m_m.shape, INIT_M, F32)
        lru_h[...] = jnp.zeros(lru_h.shape, F32)
        swa_k[...] = jnp.zeros(swa_k.shape, BF16)
        swa_v[...] = jnp.zeros(swa_v.shape, BF16)
        swa_p[...] = jnp.zeros(swa_p.shape, jnp.int32)
        h_st[...] = jnp.zeros(h_st.shape, F32)

    x = x_ref[0]
    hb = _bf(_rmsnorm(x, ng_ref[...]))

    gate_piece = d_model // 2
    n_gate_pieces = d_model // gate_piece
    widths = {"a_o": W_A, "a_z": W_A, "b_x": W_B, "b_z": W_B, "c_q": W_C, "c_kv": 2 * KV_W_C, "c_z": W_C,
              "d_q": W_D, "d_f": W_D, "d_i": W_D, "d_z": W_D, "e_q": W_E, "e_z": W_E}
    offsets = dict(_OFF, c_kv=_OFF["c_k"])
    pending = ["a_o", "a_z", "b_x", "b_z", "c_q", "c_kv"]
    pending += [("gate", j, hf) for j in range(N_BRANCH) for hf in range(n_gate_pieces)]
    pending += ["c_z", "d_f", "d_q", "d_i", "d_z", "e_q", "e_z"]
    issued = {}

    def issue(piece):
        if isinstance(piece, tuple):
            _, j, hf = piece
            g0 = _OFF["gates"] + j * d_model + hf * gate_piece
            issued[piece] = jax.nn.sigmoid(_dot(hb, win_ref[:, g0:g0 + gate_piece]))
        else:
            o = offsets[piece]
            issued[piece] = _dot(hb, win_ref[:, o:o + widths[piece]])

    def issue_pending(n):
        for _ in range(min(n, len(pending))):
            issue(pending.pop(0))

    def proj(name, width):
        if name in widths:
            assert widths[name] == width
            if name not in issued:
                pending.remove(name)
                issue(name)
            return issued[name]
        o = _OFF[name]
        return _dot(hb, win_ref[:, o:o + width])

    qk = _silu(_causal_conv(conv_a, proj("a_qk", 2 * W_A), cwa_ref, tq))
    v_a = proj("a_v", W_A)
    gif = proj("a_if", LANES) + bif_ref[...]
    logf = _log_sigmoid(gif)
    ri_a = lax.broadcasted_iota(jnp.int32, (L_A, L_A), 0)
    ci_a = lax.broadcasted_iota(jnp.int32, (L_A, L_A), 1)
    causal_a = ci_a <= ri_a
    tri_a = jnp.where(causal_a, 1.0, 0.0).astype(BF16)
    hm_chunks = []
    for n in range(tq // L_A):
        rs = slice(n * L_A, (n + 1) * L_A)
        gif_c = gif[rs]
        g_all = _cumsum_rows(tri_a, logf[rs])
        g_t = g_all.T
        gif_t = gif_c.T
        heads = []
        for h in range(N_HEAD_A):
            issue_pending(2)
            ls = slice(h * LANES, (h + 1) * LANES)
            hh, c_new, n_new, m_new = _mlstm_head(
                qk[rs, ls] * (LANES ** -0.5), qk[rs, W_A + h * LANES:W_A + (h + 1) * LANES], v_a[rs, ls],
                g_all[:, N_HEAD_A + h:N_HEAD_A + h + 1], gif_c[:, h:h + 1],
                g_t[N_HEAD_A + h:N_HEAD_A + h + 1, :], gif_t[h:h + 1, :],
                m_c[h], m_n[h, 0:1, :], m_m[h, 0:1, 0:1], causal_a)
            m_c[h] = c_new
            m_n[h] = jnp.broadcast_to(n_new, m_n.shape[1:])
            m_m[h] = jnp.broadcast_to(m_new, m_m.shape[1:])
            hh = hh * lax.rsqrt(jnp.mean(hh * hh, axis=-1, keepdims=True) + EPS)
            heads.append(hh)
        hm_chunks.append(jnp.concatenate(heads, axis=1))
    hm = jnp.concatenate(hm_chunks, axis=0) if len(hm_chunks) > 1 else hm_chunks[0]
    ya = hm * mng_ref[...] * jax.nn.sigmoid(proj("a_o", W_A)) * _silu(proj("a_z", W_A))

    xc = _causal_conv(conv_b, proj("b_x", W_B), lcw_ref, tq) + lcb_ref[...]
    issue_pending(2)
    xcb = _bf(xc)
    half = W_B // 2
    rx0 = _dot(xcb[:, :half], wlru_ref[0])
    rx1 = _dot(xcb[:, half:], wlru_ref[1])
    r = jax.nn.sigmoid(jnp.concatenate([rx0[:, :half], rx1[:, :half]], axis=1) + lba_ref[...])
    i_g = jax.nn.sigmoid(jnp.concatenate([rx0[:, half:], rx1[:, half:]], axis=1) + lbx_ref[...])
    a_t = jnp.exp(-LRU_C * r * _softplus(-lam_ref[...]))
    b_t = jnp.sqrt(1.0 - a_t * a_t) * (i_g * xc)
    issue_pending(2)
    h_lru = _linear_scan_rows(a_t, b_t, lru_h[0:1, :])
    lru_h[...] = jnp.broadcast_to(h_lru[tq - 1:tq, :], lru_h.shape)
    yb = h_lru * _silu(proj("b_z", W_B))

    q_c = proj("c_q", W_C)
    kv_c = _bf(proj("c_kv", 2 * KV_W_C))
    k_c = kv_c[:, :KV_W_C]
    v_c = kv_c[:, KV_W_C:]
    pos_c = pc_ref[0]
    pos_r = pr_ref[0]
    hd = LANES // 2
    rows2 = lax.broadcasted_iota(jnp.int32, (2 * BLK_C, 1), 0)
    ri_c = lax.broadcasted_iota(jnp.int32, (BLK_C, 2 * BLK_C), 0)
    ci_c = lax.broadcasted_iota(jnp.int32, (BLK_C, 2 * BLK_C), 1)
    rel = ri_c + BLK_C - ci_c
    in_window = (rel >= 0) & (rel < BLK_C)
    yc_blocks = []
    for n in range(tq // BLK_C):
        rs = slice(n * BLK_C, (n + 1) * BLK_C)
        k_prev = swa_k[...] if n == 0 else k_c[(n - 1) * BLK_C:n * BLK_C]
        v_prev = swa_v[...] if n == 0 else v_c[(n - 1) * BLK_C:n * BLK_C]
        p_prev = swa_p[0:1, :] if n == 0 else pos_r[:, (n - 1) * BLK_C:n * BLK_C]
        kk = jnp.concatenate([k_prev, k_c[rs]], axis=0)
        vv = jnp.concatenate([v_prev, v_c[rs]], axis=0)
        pk = jnp.concatenate([p_prev, pos_r[:, rs]], axis=1)
        dist = jnp.abs(pos_c[rs] - pk).astype(F32)
        first_key = jnp.where((c * (tq // BLK_C) + n) == 0, BLK_C, 0)
        valid = in_window & (ci_c >= first_key)
        dist2 = jnp.concatenate([dist, dist], axis=0)
        valid2 = jnp.concatenate([valid, valid], axis=0)
        kdup = [jnp.concatenate([kk[:, j * hd:(j + 1) * hd]] * 2, axis=1) for j in range(2)]
        vdup = [jnp.concatenate([vv[:, j * hd:(j + 1) * hd]] * 2, axis=1) for j in range(2)]
        pairs = []
        for p in range(N_HEAD_C // 2):
            issue_pending(1)
            kvh = (2 * p) // (N_HEAD_C // 2)
            top = rows2 < BLK_C
            slope = jnp.where(top, 2.0 ** (-(2 * p + 1)), 2.0 ** (-(2 * p + 2)))
            sink = jnp.where(top, sink_ref[0, 2 * p], sink_ref[0, 2 * p + 1])

            def bias(sc, slope=slope, dist2=dist2, valid2=valid2):
                return jnp.where(valid2, sc - slope * dist2, -jnp.inf)

            pairs.append(_pair_attention(q_c[rs, p * LANES:(p + 1) * LANES], kdup[kvh], vdup[kvh],
                                         hd ** -0.5, bias, sink))
        yc_blocks.append(jnp.concatenate(pairs, axis=1))
    last = slice(tq - BLK_C, tq)
    swa_k[...] = k_c[last]
    swa_v[...] = v_c[last]
    swa_p[...] = jnp.broadcast_to(pos_r[:, last], swa_p.shape)
    yc = (jnp.concatenate(yc_blocks, axis=0) if len(yc_blocks) > 1 else yc_blocks[0]) * _silu(proj("c_z", W_C))

    lbp = hlb_ref[...]
    e_lb = jnp.exp(lbp - jnp.max(lbp, axis=0, keepdims=True))
    p_lb = e_lb / jnp.sum(e_lb, axis=0, keepdims=True)
    lb = jnp.zeros((1, W_D), F32)
    for j in range(1, layer + 1):
        lb = lb + p_lb[j:j + 1, :]
    ls_f = jnp.log1p(-lb) + _log_sigmoid(proj("d_f", W_D))
    log_lb = jnp.log(lb)
    mx = jnp.maximum(log_lb, ls_f)
    log_f = mx + jnp.log(1.0 + jnp.exp(-jnp.abs(log_lb - ls_f)))
    k_d = 1.0 - jnp.exp(log_f)
    qs_d = _silu(proj("d_q", W_D)) * (LANES ** -0.5)
    v_d = proj("d_i", W_D)
    ri_d = lax.broadcasted_iota(jnp.int32, (tq, tq), 0)
    ci_d = lax.broadcasted_iota(jnp.int32, (tq, tq), 1)
    tri_d = jnp.where(ci_d <= ri_d, 1.0, 0.0).astype(BF16)
    b_all = _cumsum_rows(tri_d, log_f)
    hd_heads = []
    for h in range(N_HEAD_D):
        ls = slice(h * LANES, (h + 1) * LANES)
        o_h, st_new = _hgrn_head(qs_d[:, ls], k_d[:, ls], v_d[:, ls], b_all[:, ls], h_st[h], ri_d, ci_d)
        h_st[h] = st_new
        hd_heads.append(o_h * lax.rsqrt(jnp.mean(o_h * o_h, axis=-1, keepdims=True) + EPS))
    yd = jnp.concatenate(hd_heads, axis=1) * hng_ref[...] * _silu(proj("d_z", W_D))

    q_e = proj("e_q", W_E)
    ye = jnp.concatenate(
        [_pair_attention(q_e[:, p * LANES:(p + 1) * LANES], kv_ref[:, p * LANES:(p + 1) * LANES],
                         kv_ref[:, W_E + p * LANES:W_E + (p + 1) * LANES], hd ** -0.5, lambda sc: sc, None)
         for p in range(W_E // LANES)], axis=1) * _silu(proj("e_z", W_E))

    issue_pending(len(pending))
    y_pieces = []
    for hf in range(n_gate_pieces):
        cs = slice(hf * gate_piece, (hf + 1) * gate_piece)
        y = None
        row0 = 0
        for j, (yj, wj) in enumerate(zip((ya, yb, yc, yd, ye), BRANCH_W)):
            t_j = issued[("gate", j, hf)] * _dot(_bf(yj), wbr_ref[row0:row0 + wj, cs])
            y = t_j if y is None else y + t_j
            row0 += wj
        y_pieces.append(_bf(y))
    x_new = x + _dot(jnp.concatenate(y_pieces, axis=1), wout_ref[...])
    if final:
        x_new = _rmsnorm(x_new, fg_ref[...])
    o_ref[0] = x_new


def _const_spec(shape):
    nd = len(shape)
    return pl.BlockSpec(shape, lambda b, c: (0,) * nd, pipeline_mode=pl.Buffered(1))


_STACKED = ("w_in", "w_lru", "w_br", "w_out")


def _layer_spec(shape, layer):
    nd = len(shape)
    return pl.BlockSpec((None,) + tuple(shape[1:]), lambda b, c: (layer,) + (0,) * (nd - 1),
                        pipeline_mode=pl.Buffered(1))


def _layer_call(x, pos_col, pos_row, mem_kv, params, *, layer, final, tq):
    bsz, seq, d_model = x.shape
    in_specs = [
        pl.BlockSpec((1, tq, d_model), lambda b, c: (b, c, 0)),
        pl.BlockSpec((1, tq, 1), lambda b, c: (b, c, 0)),
        pl.BlockSpec((1, 1, tq), lambda b, c: (b, 0, c)),
        pl.BlockSpec((None, None) + tuple(mem_kv.shape[2:]), lambda b, c: (layer, b, 0, 0)),
    ]
    for name, p in params:
        if name == "sinks":
            in_specs.append(pl.BlockSpec(memory_space=pltpu.SMEM))
        elif name in _STACKED:
            in_specs.append(_layer_spec(p.shape, layer))
        else:
            in_specs.append(_const_spec(p.shape))
    scratch = [
        pltpu.VMEM((tq + SUBLANES, 2 * W_A), F32),
        pltpu.VMEM((tq + SUBLANES, W_B), F32),
        pltpu.VMEM((N_HEAD_A, LANES, LANES), F32),
        pltpu.VMEM((N_HEAD_A, SUBLANES, LANES), F32),
        pltpu.VMEM((N_HEAD_A, SUBLANES, LANES), F32),
        pltpu.VMEM((SUBLANES, W_B), F32),
        pltpu.VMEM((BLK_C, KV_W_C), BF16),
        pltpu.VMEM((BLK_C, KV_W_C), BF16),
        pltpu.VMEM((SUBLANES, BLK_C), jnp.int32),
        pltpu.VMEM((N_HEAD_D, LANES, LANES), F32),
    ]
    return pl.pallas_call(
        functools.partial(_layer_kernel, layer=layer, final=final, tq=tq),
        grid=(bsz, seq // tq),
        in_specs=in_specs,
        out_specs=pl.BlockSpec((1, tq, d_model), lambda b, c: (b, c, 0)),
        out_shape=jax.ShapeDtypeStruct(x.shape, x.dtype),
        scratch_shapes=scratch,
        compiler_params=pltpu.CompilerParams(dimension_semantics=("arbitrary", "arbitrary"),
                                             vmem_limit_bytes=VMEM_LIMIT_BYTES),
        name=f"hybrid_layer_{layer}",
    )(x, pos_col, pos_row, mem_kv, *[p for _, p in params])


def _tile_rows(seq):
    for t in (256, 128):
        if seq % t == 0:
            return t
    raise ValueError("sequence length must be a multiple of 128")


def kernel(x, mem, positions, norm_g, w_in, mlstm_conv_w, mlstm_b_if, mlstm_norm_g, lru_conv_w, lru_conv_b, lru_wa, lru_ba, lru_wx, lru_bx, lru_lambda, swa_sinks, hgrn_lb, hgrn_norm_g, mem_norm_g, w_mem_kv, w_br, w_out, final_norm_g):
    depth, d_model, n_in = w_in.shape
    n_a = 5 * W_A
    assert n_in == n_a + N_IF + 2 * W_B + 2 * W_C + 2 * KV_W_C + 4 * W_D + 2 * W_E + N_BRANCH * d_model
    tq = _tile_rows(x.shape[1])

    w_in_p = jnp.concatenate([w_in[:, :, :n_a + N_IF].astype(BF16), jnp.zeros((depth, d_model, LANES - N_IF), BF16),
                              w_in[:, :, n_a + N_IF:].astype(BF16)], axis=-1)
    b_if = jnp.pad(mlstm_b_if, ((0, 0), (0, LANES - N_IF)))[:, None, :]
    nb, bd = lru_wa.shape[1], lru_wa.shape[2]
    eye = jnp.eye(nb, dtype=lru_wa.dtype)

    def block_diag(w):
        return (eye[None, :, None, :, None] * w[:, :, :, None, :]).reshape(depth, nb * bd, nb * bd)

    wa_d, wx_d = block_diag(lru_wa), block_diag(lru_wx)
    hw = W_B // 2
    w_lru = jnp.stack([jnp.concatenate([wa_d[:, :hw, :hw], wx_d[:, :hw, :hw]], axis=-1),
                       jnp.concatenate([wa_d[:, hw:, hw:], wx_d[:, hw:, hw:]], axis=-1)], axis=1).astype(BF16)
    mem_kv = _mem_kv_call(mem, mem_norm_g, w_mem_kv.astype(BF16))
    w_br_b = w_br.astype(BF16)
    w_out_b = w_out.astype(BF16)

    pos_col = positions[:, :, None]
    pos_row = positions[:, None, :]
    row = lambda a, l: a[l][None, :]
    for l in range(depth):
        params = [
            ("norm_g", row(norm_g, l)), ("w_in", w_in_p), ("conv_a", mlstm_conv_w[l]), ("b_if", b_if[l]),
            ("mlstm_norm_g", row(mlstm_norm_g, l)), ("lru_conv_w", lru_conv_w[l]), ("lru_conv_b", row(lru_conv_b, l)),
            ("w_lru", w_lru), ("lru_ba", row(lru_ba, l)), ("lru_bx", row(lru_bx, l)), ("lru_lambda", row(lru_lambda, l)),
            ("sinks", row(swa_sinks, l)), ("hgrn_lb", hgrn_lb), ("hgrn_norm_g", row(hgrn_norm_g, l)),
            ("w_br", w_br_b), ("w_out", w_out_b), ("final_norm_g", final_norm_g[None, :]),
        ]
        x = _layer_call(x, pos_col, pos_row, mem_kv, params, layer=l, final=(l == depth - 1), tq=tq)
    return x
```

```python
import functools
import math

import jax
import jax.numpy as jnp
from jax import lax
from jax.experimental import pallas as pl
from jax.experimental.pallas import tpu as pltpu

F32 = jnp.float32
BF16 = jnp.bfloat16

EPS = 1e-6
CONV_W = 4
N_HEAD_A = 4
W_A = 512
L_A = 128
INIT_M = -1e30
W_B = 512
LRU_C = 8.0
N_HEAD_C = 8
W_C = 512
KV_W_C = 128
BLK_C = 128
N_HEAD_D = 4
W_D = 512
DIAG_D = 8
W_E = 256
N_BRANCH = 5
BRANCH_W = (W_A, W_B, W_C, W_D, W_E)

LANES = 128
SUBLANES = 8
VMEM_LIMIT_BYTES = 58 * 1024 * 1024

_OFF = {}
_acc = 0
for _name, _w in (("a_qk", 2 * W_A), ("a_v", W_A), ("a_o", W_A), ("a_z", W_A), ("a_if", LANES),
                  ("b_x", W_B), ("b_z", W_B),
                  ("c_q", W_C), ("c_k", KV_W_C), ("c_v", KV_W_C), ("c_z", W_C),
                  ("d_q", W_D), ("d_f", W_D), ("d_i", W_D), ("d_z", W_D),
                  ("e_q", W_E), ("e_z", W_E), ("gates", None)):
    _OFF[_name] = _acc
    if _w is not None:
        _acc += _w
N_IF = 2 * N_HEAD_A


def _bf(x):
    return x.astype(BF16)


def _dot(a, b):
    return jnp.dot(a, b, preferred_element_type=F32)


def _dot_nt(a, b):
    return lax.dot_general(a, b, (((1,), (1,)), ((), ())), preferred_element_type=F32)


def _dot_tn(a, b):
    return lax.dot_general(a, b, (((0,), (0,)), ((), ())), preferred_element_type=F32)


def _silu(x):
    return x * jax.nn.sigmoid(x)


def _log_sigmoid(x):
    return jnp.minimum(x, 0.0) - jnp.log(1.0 + jnp.exp(-jnp.abs(x)))


def _softplus(x):
    return jnp.maximum(x, 0.0) + jnp.log(1.0 + jnp.exp(-jnp.abs(x)))


def _linear_scan_rows(a, b, h0):
    t, w = a.shape
    groups = t // SUBLANES
    a3 = a.reshape(groups, SUBLANES, w)
    b3 = b.reshape(groups, SUBLANES, w)
    sub = lax.broadcasted_iota(jnp.int32, a3.shape, 1)
    s = 1
    while s < SUBLANES:
        keep = sub >= s
        a_s = jnp.where(keep, pltpu.roll(a3, s, 1), 1.0)
        b_s = jnp.where(keep, pltpu.roll(b3, s, 1), 0.0)
        b3 = a3 * b_s + b3
        a3 = a3 * a_s
        s *= 2
    yield
    carry = h0
    out = []
    for g in range(groups):
        hg = b3[g] + a3[g] * carry
        out.append(hg)
        carry = hg[SUBLANES - 1:SUBLANES, :]
        if g % SUBLANES == SUBLANES - 1:
            yield
    return jnp.concatenate(out, axis=0)


def _rmsnorm(x, g):
    return x * lax.rsqrt(jnp.mean(x * x, axis=-1, keepdims=True) + EPS) * g


def _cumsum_rows(tri, x):
    hi = _bf(x)
    r1 = x - hi.astype(F32)
    mid = _bf(r1)
    lo = _bf(r1 - mid.astype(F32))
    return _dot(tri, hi) + _dot(tri, mid) + _dot(tri, lo)


def _block_row_bcast(b, blk, idx):
    n = b.shape[0] // blk
    b3 = b.reshape(n, blk, b.shape[1])
    return jnp.broadcast_to(b3[:, idx:idx + 1, :], b3.shape).reshape(b.shape)


def _causal_conv(buf, x, w_ref, t):
    buf[pl.ds(SUBLANES, t), :] = x
    y = w_ref[0:1, :] * buf[pl.ds(SUBLANES - 3, t), :]
    for k in range(1, CONV_W):
        y = y + w_ref[k:k + 1, :] * buf[pl.ds(SUBLANES - 3 + k, t), :]
    buf[pl.ds(0, SUBLANES), :] = buf[pl.ds(t, SUBLANES), :]
    return y


def _mlstm_head(q, k, v, g, icol, grow, irow, c_st, n_st, m_st, causal):
    L = q.shape[0]
    dm = jnp.where(causal, g - grow + irow, -jnp.inf)
    g_end = g[L - 1:L, :]
    w_end = g_end - g + icol
    m_loc = jnp.max(w_end, axis=0, keepdims=True)
    ke = k * jnp.exp(w_end - m_loc)
    vb = _bf(v)
    c_loc = _dot_tn(_bf(ke), vb)
    n_loc = jnp.sum(ke, axis=0, keepdims=True)

    a_inter = g + m_st
    m_j = jnp.maximum(a_inter, jnp.max(dm, axis=-1, keepdims=True))
    qb = _bf(q)
    p = jnp.exp(dm - m_j) * _dot_nt(qb, _bf(k))
    w_inter = jnp.exp(a_inter - m_j)
    num = _dot(_bf(p), vb) + w_inter * _dot(qb, _bf(c_st))
    den = jnp.sum(p, axis=-1, keepdims=True) + w_inter * jnp.sum(q * n_st, axis=-1, keepdims=True)
    h = num / jnp.maximum(jnp.abs(den), jnp.exp(-m_j))

    m_new = jnp.maximum(g_end + m_st, m_loc)
    a = jnp.exp(g_end + m_st - m_new)
    b = jnp.exp(m_loc - m_new)
    return h, a * c_st + b * c_loc, a * n_st + b * n_loc, m_new


def _hgrn_head(qs, kk, v, b, st_t, ri, ci):
    t = qs.shape[0]
    rows = ri[:, 0:1]
    o = _dot_nt(_bf(qs * jnp.exp(b)), _bf(st_t))
    b_end = b[t - 1:t, :]
    vb = _bf(v)
    st_new = st_t * jnp.exp(b_end) + _dot_tn(vb, _bf(kk * jnp.exp(b_end - b)))
    yield

    amat = None
    s = t // 2
    while s >= DIAG_D:
        sh = (2 * s).bit_length() - 1
        e = jnp.exp(-jnp.abs(b - _block_row_bcast(b, 2 * s, s - 1)))
        upper = (rows & (2 * s - 1)) >= s
        ql = qs * jnp.where(upper, e, 0.0)
        kl = kk * jnp.where(upper, 0.0, e)
        al = _dot_nt(_bf(ql), _bf(kl))
        same = (ri >> sh) == (ci >> sh)
        amat = jnp.where(same, al, 0.0 if amat is None else amat)
        s //= 2
        yield
    sh = DIAG_D.bit_length() - 1
    d = b - _block_row_bcast(b, DIAG_D, 0)
    ad = _dot_nt(_bf(qs * jnp.exp(d)), _bf(kk * jnp.exp(-d)))
    diag = ((ri >> sh) == (ci >> sh)) & (ci <= ri)
    amat = jnp.where(diag, ad, amat)
    return o + _dot(_bf(amat), vb), st_new


def _pair_attention(qp, kmat, vmat, scale, bias_fn, sink_col):
    t = qp.shape[0]
    lane = lax.broadcasted_iota(jnp.int32, qp.shape, 1)
    lo = lane < (LANES // 2)
    assert math.log2(scale).is_integer()
    qs = qp * scale
    q2 = jnp.concatenate([jnp.where(lo, qs, 0.0), jnp.where(lo, 0.0, qs)], axis=0)
    s = bias_fn(_dot_nt(_bf(q2), kmat))
    m = jnp.max(s, axis=-1, keepdims=True)
    if sink_col is not None:
        m = jnp.maximum(m, sink_col)
    p = jnp.exp(s - m)
    den = jnp.sum(p, axis=-1, keepdims=True)
    if sink_col is not None:
        den = den + jnp.exp(sink_col - m)
    o2 = _dot(_bf(p), vmat) / den
    return jnp.where(lo, o2[:t], o2[t:])


def _run_with_fill(tasks, issue):
    for group in tasks:
        active = [(gen, pieces, [0]) for gen, pieces in group]
        while active:
            for task in list(active):
                gen, pieces, count = task
                try:
                    next(gen)
                except StopIteration:
                    active.remove(task)
                    continue
                if count[0] < len(pieces):
                    issue(pieces[count[0]])
                count[0] += 1


def _w_in_prep_kernel(w_ref, o_ref, *, n_a):
    o_ref[:, :n_a] = _bf(w_ref[:, :n_a])
    t = w_ref[:, n_a:n_a + LANES]
    lane = lax.broadcasted_iota(jnp.int32, t.shape, 1)
    o_ref[:, n_a:n_a + LANES] = _bf(jnp.where(lane < N_IF, t, 0.0))
    o_ref[:, n_a + LANES:] = _bf(w_ref[:, n_a + N_IF:])


def _w_in_prep_call(w_in, n_a):
    depth, d_model, n_in = w_in.shape
    rows = LANES
    return pl.pallas_call(
        functools.partial(_w_in_prep_kernel, n_a=n_a),
        grid=(depth, d_model // rows),
        in_specs=[pl.BlockSpec((None, rows, n_in), lambda l, r: (l, r, 0))],
        out_specs=pl.BlockSpec((None, rows, n_in + LANES - N_IF), lambda l, r: (l, r, 0)),
        out_shape=jax.ShapeDtypeStruct((depth, d_model, n_in + LANES - N_IF), BF16),
        compiler_params=pltpu.CompilerParams(dimension_semantics=("arbitrary", "arbitrary"),
                                             vmem_limit_bytes=VMEM_LIMIT_BYTES),
        name="w_in_prep",
    )(w_in)


def _mem_kv_kernel(mem_ref, g_ref, w_ref, o_ref):
    o_ref[...] = _bf(_dot(_bf(_rmsnorm(mem_ref[0], g_ref[...])), w_ref[...]))


def _mem_kv_call(mem, mem_norm_g, w_kv):
    bsz, m_len, d_model = mem.shape
    depth, _, n_kv = w_kv.shape
    return pl.pallas_call(
        _mem_kv_kernel,
        grid=(depth, bsz),
        in_specs=[pl.BlockSpec((1, m_len, d_model), lambda l, b: (b, 0, 0)),
                  pl.BlockSpec((None, 1, d_model), lambda l, b: (l, 0, 0)),
                  pl.BlockSpec((None, d_model, n_kv), lambda l, b: (l, 0, 0))],
        out_specs=pl.BlockSpec((None, None, m_len, n_kv), lambda l, b: (l, b, 0, 0)),
        out_shape=jax.ShapeDtypeStruct((depth, bsz, m_len, n_kv), BF16),
        compiler_params=pltpu.CompilerParams(dimension_semantics=("arbitrary", "arbitrary")),
        name="mem_kv",
    )(mem, mem_norm_g[:, None, :], w_kv)


def _layer_kernel(x_ref, pc_ref, pr_ref, kv_ref, ng_ref, win_ref, cwa_ref, bif_ref, mng_ref,
                  lcw_ref, lcb_ref, wlru_ref, lba_ref, lbx_ref, lam_ref, sink_ref, hlb_ref, hng_ref,
                  wbr_ref, wout_ref, fg_ref,
                  o_ref,
                  conv_a, conv_b, m_c, m_n, m_m, lru_h, swa_k, swa_v, swa_p, h_st,
                  *, layer, final, tq):
    c = pl.program_id(1)
    d_model = x_ref.shape[-1]

    @pl.when(c == 0)
    def _init():
        conv_a[pl.ds(0, SUBLANES), :] = jnp.zeros((SUBLANES, conv_a.shape[1]), F32)
        conv_b[pl.ds(0, SUBLANES), :] = jnp.zeros((SUBLANES, conv_b.shape[1]), F32)
        m_c[...] = jnp.zeros(m_c.shape, F32)
        m_n[...] = jnp.zeros(m_n.shape, F32)
        m_m[...] = jnp.full(m_m.shape, INIT_M, F32)
        lru_h[...] = jnp.zeros(lru_h.shape, F32)
        swa_k[...] = jnp.zeros(swa_k.shape, BF16)
        swa_v[...] = jnp.zeros(swa_v.shape, BF16)
        swa_p[...] = jnp.zeros(swa_p.shape, jnp.int32)
        h_st[...] = jnp.zeros(h_st.shape, F32)

    x = x_ref[0]
    hb = _bf(_rmsnorm(x, ng_ref[...]))

    gate_piece = d_model // 2
    n_gate_pieces = d_model // gate_piece
    widths = {"a_o": W_A, "a_z": W_A, "b_x": W_B, "b_z": W_B, "c_q": W_C, "c_kv": 2 * KV_W_C, "c_z": W_C,
              "d_q": W_D, "d_f": W_D, "d_i": W_D, "d_z": W_D, "e_q": W_E, "e_z": W_E}
    offsets = dict(_OFF, c_kv=_OFF["c_k"])
    gates = [("gate", j, hf) for j in range(N_BRANCH) for hf in range(n_gate_pieces)]
    pending = (["a_o", "a_z", "b_x", "b_z", "c_q", "c_kv"] + gates[:2] + ["c_z", "d_f", "d_q", "d_i"]
               + ["d_z", "e_q", "e_z"] + gates[2:])
    issued = {}

    def issue(piece):
        if isinstance(piece, tuple):
            _, j, hf = piece
            g0 = _OFF["gates"] + j * d_model + hf * gate_piece
            issued[piece] = jax.nn.sigmoid(_dot(hb, win_ref[:, g0:g0 + gate_piece]))
        else:
            o = offsets[piece]
            issued[piece] = _dot(hb, win_ref[:, o:o + widths[piece]])

    def issue_pending(n):
        for _ in range(min(n, len(pending))):
            issue(pending.pop(0))

    def proj(name, width):
        if name in widths:
            assert widths[name] == width
            if name not in issued:
                pending.remove(name)
                issue(name)
            return issued[name]
        o = _OFF[name]
        return _dot(hb, win_ref[:, o:o + width])

    branch_out = {}
    hd = LANES // 2

    def mixer_a():
        qk = _silu(_causal_conv(conv_a, proj("a_qk", 2 * W_A), cwa_ref, tq))
        yield
        v_a = proj("a_v", W_A)
        gif = proj("a_if", LANES) + bif_ref[...]
        logf = _log_sigmoid(gif)
        ri_a = lax.broadcasted_iota(jnp.int32, (L_A, L_A), 0)
        ci_a = lax.broadcasted_iota(jnp.int32, (L_A, L_A), 1)
        causal_a = ci_a <= ri_a
        tri_a = jnp.where(causal_a, 1.0, 0.0).astype(BF16)
        hm_chunks = []
        for n in range(tq // L_A):
            rs = slice(n * L_A, (n + 1) * L_A)
            gif_c = gif[rs]
            g_all = _cumsum_rows(tri_a, logf[rs])
            g_t = g_all.T
            gif_t = gif_c.T
            heads = []
            for h in range(N_HEAD_A):
                ls = slice(h * LANES, (h + 1) * LANES)
                hh, c_new, n_new, m_new = _mlstm_head(
                    qk[rs, ls] * (LANES ** -0.5), qk[rs, W_A + h * LANES:W_A + (h + 1) * LANES], v_a[rs, ls],
                    g_all[:, N_HEAD_A + h:N_HEAD_A + h + 1], gif_c[:, h:h + 1],
                    g_t[N_HEAD_A + h:N_HEAD_A + h + 1, :], gif_t[h:h + 1, :],
                    m_c[h], m_n[h, 0:1, :], m_m[h, 0:1, 0:1], causal_a)
                m_c[h] = c_new
                m_n[h] = jnp.broadcast_to(n_new, m_n.shape[1:])
                m_m[h] = jnp.broadcast_to(m_new, m_m.shape[1:])
                hh = hh * lax.rsqrt(jnp.mean(hh * hh, axis=-1, keepdims=True) + EPS)
                heads.append(hh)
                yield
            hm_chunks.append(jnp.concatenate(heads, axis=1))
        hm = jnp.concatenate(hm_chunks, axis=0) if len(hm_chunks) > 1 else hm_chunks[0]
        branch_out["a"] = hm * mng_ref[...] * jax.nn.sigmoid(proj("a_o", W_A)) * _silu(proj("a_z", W_A))

    def mixer_b():
        xc = _causal_conv(conv_b, proj("b_x", W_B), lcw_ref, tq) + lcb_ref[...]
        yield
        xcb = _bf(xc)
        half = W_B // 2
        rx0 = _dot(xcb[:, :half], wlru_ref[0])
        rx1 = _dot(xcb[:, half:], wlru_ref[1])
        r = jax.nn.sigmoid(jnp.concatenate([rx0[:, :half], rx1[:, :half]], axis=1) + lba_ref[...])
        i_g = jax.nn.sigmoid(jnp.concatenate([rx0[:, half:], rx1[:, half:]], axis=1) + lbx_ref[...])
        yield
        a_t = jnp.exp(-LRU_C * r * _softplus(-lam_ref[...]))
        b_t = jnp.sqrt(1.0 - a_t * a_t) * (i_g * xc)
        yield
        h_lru = yield from _linear_scan_rows(a_t, b_t, lru_h[0:1, :])
        lru_h[...] = jnp.broadcast_to(h_lru[tq - 1:tq, :], lru_h.shape)
        branch_out["b"] = h_lru * _silu(proj("b_z", W_B))

    def mixer_c():
        q_c = proj("c_q", W_C)
        kv_c = _bf(proj("c_kv", 2 * KV_W_C))
        k_c = kv_c[:, :KV_W_C]
        v_c = kv_c[:, KV_W_C:]
        pos_c = pc_ref[0]
        pos_r = pr_ref[0]
        rows2 = lax.broadcasted_iota(jnp.int32, (2 * BLK_C, 1), 0)
        ri_c = lax.broadcasted_iota(jnp.int32, (BLK_C, 2 * BLK_C), 0)
        ci_c = lax.broadcasted_iota(jnp.int32, (BLK_C, 2 * BLK_C), 1)
        rel = ri_c + BLK_C - ci_c
        in_window = (rel >= 0) & (rel < BLK_C)
        yc_blocks = []
        for n in range(tq // BLK_C):
            rs = slice(n * BLK_C, (n + 1) * BLK_C)
            k_prev = swa_k[...] if n == 0 else k_c[(n - 1) * BLK_C:n * BLK_C]
            v_prev = swa_v[...] if n == 0 else v_c[(n - 1) * BLK_C:n * BLK_C]
            p_prev = swa_p[0:1, :] if n == 0 else pos_r[:, (n - 1) * BLK_C:n * BLK_C]
            kk = jnp.concatenate([k_prev, k_c[rs]], axis=0)
            vv = jnp.concatenate([v_prev, v_c[rs]], axis=0)
            pk = jnp.concatenate([p_prev, pos_r[:, rs]], axis=1)
            dist = jnp.abs(pos_c[rs] - pk).astype(F32)
            first_key = jnp.where((c * (tq // BLK_C) + n) == 0, BLK_C, 0)
            valid = in_window & (ci_c >= first_key)
            dist2 = jnp.concatenate([dist, dist], axis=0)
            valid2 = jnp.concatenate([valid, valid], axis=0)
            kdup = [jnp.concatenate([kk[:, j * hd:(j + 1) * hd]] * 2, axis=1) for j in range(2)]
            vdup = [jnp.concatenate([vv[:, j * hd:(j + 1) * hd]] * 2, axis=1) for j in range(2)]
            pairs = []
            for p in range(N_HEAD_C // 2):
                kvh = (2 * p) // (N_HEAD_C // 2)
                top = rows2 < BLK_C
                slope = jnp.where(top, 2.0 ** (-(2 * p + 1)), 2.0 ** (-(2 * p + 2)))
                sink = jnp.where(top, sink_ref[0, 2 * p], sink_ref[0, 2 * p + 1])

                def bias(sc, slope=slope, dist2=dist2, valid2=valid2):
                    return jnp.where(valid2, sc - slope * dist2, -jnp.inf)

                pairs.append(_pair_attention(q_c[rs, p * LANES:(p + 1) * LANES], kdup[kvh], vdup[kvh],
                                             hd ** -0.5, bias, sink))
                yield
            yc_blocks.append(jnp.concatenate(pairs, axis=1))
        last = slice(tq - BLK_C, tq)
        swa_k[...] = k_c[last]
        swa_v[...] = v_c[last]
        swa_p[...] = jnp.broadcast_to(pos_r[:, last], swa_p.shape)
        yc = jnp.concatenate(yc_blocks, axis=0) if len(yc_blocks) > 1 else yc_blocks[0]
        branch_out["c"] = yc * _silu(proj("c_z", W_C))

    def mixer_d():
        lbp = hlb_ref[...]
        e_lb = jnp.exp(lbp - jnp.max(lbp, axis=0, keepdims=True))
        p_lb = e_lb / jnp.sum(e_lb, axis=0, keepdims=True)
        lb = jnp.zeros((1, W_D), F32)
        for j in range(1, layer + 1):
            lb = lb + p_lb[j:j + 1, :]
        ls_f = jnp.log1p(-lb) + _log_sigmoid(proj("d_f", W_D))
        log_lb = jnp.log(lb)
        mx = jnp.maximum(log_lb, ls_f)
        log_f = mx + jnp.log(1.0 + jnp.exp(-jnp.abs(log_lb - ls_f)))
        k_d = 1.0 - jnp.exp(log_f)
        yield
        ri_d = lax.broadcasted_iota(jnp.int32, (tq, tq), 0)
        ci_d = lax.broadcasted_iota(jnp.int32, (tq, tq), 1)
        tri_d = jnp.where(ci_d <= ri_d, 1.0, 0.0).astype(BF16)
        b_all = _cumsum_rows(tri_d, log_f)
        qs_d = _silu(proj("d_q", W_D)) * (LANES ** -0.5)
        v_d = proj("d_i", W_D)
        yield
        hd_heads = []
        for h in range(N_HEAD_D):
            ls = slice(h * LANES, (h + 1) * LANES)
            o_h, st_new = yield from _hgrn_head(qs_d[:, ls], k_d[:, ls], v_d[:, ls], b_all[:, ls], h_st[h],
                                                ri_d, ci_d)
            h_st[h] = st_new
            hd_heads.append(o_h * lax.rsqrt(jnp.mean(o_h * o_h, axis=-1, keepdims=True) + EPS))
            yield
        branch_out["d"] = jnp.concatenate(hd_heads, axis=1) * hng_ref[...] * _silu(proj("d_z", W_D))

    def mixer_e():
        q_e = proj("e_q", W_E)
        pairs = []
        for p in range(W_E // LANES):
            ls = slice(p * LANES, (p + 1) * LANES)
            pairs.append(_pair_attention(q_e[:, ls], kv_ref[:, ls], kv_ref[:, W_E + p * LANES:W_E + (p + 1) * LANES],
                                         hd ** -0.5, lambda sc: sc, None))
            yield
        branch_out["e"] = jnp.concatenate(pairs, axis=1) * _silu(proj("e_z", W_E))

    yields_per_head_d = (tq // 2 // DIAG_D).bit_length() + 2
    heads_d = [0, 0] + ([1] + [0] * (yields_per_head_d - 1)) * N_HEAD_D
    _run_with_fill([[(mixer_a(), [0] + [1] * (N_HEAD_A * (tq // L_A)))],
                    [(mixer_b(), [1, 1, 1, 1])],
                    [(mixer_c(), [1] * (N_HEAD_C // 2 * (tq // BLK_C)))],
                    [(mixer_d(), heads_d)],
                    [(mixer_e(), [])]], issue_pending)

    issue_pending(len(pending))
    y_pieces = []
    for hf in range(n_gate_pieces):
        cs = slice(hf * gate_piece, (hf + 1) * gate_piece)
        y = None
        row0 = 0
        for j, (name, wj) in enumerate(zip("abcde", BRANCH_W)):
            t_j = issued[("gate", j, hf)] * _dot(_bf(branch_out[name]), wbr_ref[row0:row0 + wj, cs])
            y = t_j if y is None else y + t_j
            row0 += wj
        y_pieces.append(_bf(y))
    x_new = x + _dot(jnp.concatenate(y_pieces, axis=1), wout_ref[...])
    if final:
        x_new = _rmsnorm(x_new, fg_ref[...])
    o_ref[0] = x_new


def _const_spec(shape):
    nd = len(shape)
    return pl.BlockSpec(shape, lambda b, c: (0,) * nd, pipeline_mode=pl.Buffered(1))


_STACKED = ("w_in", "w_lru", "w_br", "w_out")


def _layer_spec(shape, layer):
    nd = len(shape)
    return pl.BlockSpec((None,) + tuple(shape[1:]), lambda b, c: (layer,) + (0,) * (nd - 1),
                        pipeline_mode=pl.Buffered(1))


def _layer_call(x, pos_col, pos_row, mem_kv, params, *, layer, final, tq):
    bsz, seq, d_model = x.shape
    in_specs = [
        pl.BlockSpec((1, tq, d_model), lambda b, c: (b, c, 0)),
        pl.BlockSpec((1, tq, 1), lambda b, c: (b, c, 0)),
        pl.BlockSpec((1, 1, tq), lambda b, c: (b, 0, c)),
        pl.BlockSpec((None, None) + tuple(mem_kv.shape[2:]), lambda b, c: (layer, b, 0, 0)),
    ]
    for name, p in params:
        if name == "sinks":
            in_specs.append(pl.BlockSpec(memory_space=pltpu.SMEM))
        elif name in _STACKED:
            in_specs.append(_layer_spec(p.shape, layer))
        else:
            in_specs.append(_const_spec(p.shape))
    scratch = [
        pltpu.VMEM((tq + SUBLANES, 2 * W_A), F32),
        pltpu.VMEM((tq + SUBLANES, W_B), F32),
        pltpu.VMEM((N_HEAD_A, LANES, LANES), F32),
        pltpu.VMEM((N_HEAD_A, SUBLANES, LANES), F32),
        pltpu.VMEM((N_HEAD_A, SUBLANES, LANES), F32),
        pltpu.VMEM((SUBLANES, W_B), F32),
        pltpu.VMEM((BLK_C, KV_W_C), BF16),
        pltpu.VMEM((BLK_C, KV_W_C), BF16),
        pltpu.VMEM((SUBLANES, BLK_C), jnp.int32),
        pltpu.VMEM((N_HEAD_D, LANES, LANES), F32),
    ]
    return pl.pallas_call(
        functools.partial(_layer_kernel, layer=layer, final=final, tq=tq),
        grid=(bsz, seq // tq),
        in_specs=in_specs,
        out_specs=pl.BlockSpec((1, tq, d_model), lambda b, c: (b, c, 0)),
        out_shape=jax.ShapeDtypeStruct(x.shape, x.dtype),
        scratch_shapes=scratch,
        compiler_params=pltpu.CompilerParams(dimension_semantics=("arbitrary", "arbitrary"),
                                             vmem_limit_bytes=VMEM_LIMIT_BYTES),
        name=f"hybrid_layer_{layer}",
    )(x, pos_col, pos_row, mem_kv, *[p for _, p in params])


def _tile_rows(seq):
    for t in (256, 128):
        if seq % t == 0:
            return t
    raise ValueError("sequence length must be a multiple of 128")


def kernel(x, mem, positions, norm_g, w_in, mlstm_conv_w, mlstm_b_if, mlstm_norm_g, lru_conv_w, lru_conv_b, lru_wa, lru_ba, lru_wx, lru_bx, lru_lambda, swa_sinks, hgrn_lb, hgrn_norm_g, mem_norm_g, w_mem_kv, w_br, w_out, final_norm_g):
    depth, d_model, n_in = w_in.shape
    n_a = 5 * W_A
    assert n_in == n_a + N_IF + 2 * W_B + 2 * W_C + 2 * KV_W_C + 4 * W_D + 2 * W_E + N_BRANCH * d_model
    tq = _tile_rows(x.shape[1])

    w_in_p = _w_in_prep_call(w_in, n_a)
    b_if = jnp.pad(mlstm_b_if, ((0, 0), (0, LANES - N_IF)))[:, None, :]
    nb, bd = lru_wa.shape[1], lru_wa.shape[2]
    eye = jnp.eye(nb, dtype=lru_wa.dtype)

    def block_diag(w):
        return (eye[None, :, None, :, None] * w[:, :, :, None, :]).reshape(depth, nb * bd, nb * bd)

    wa_d, wx_d = block_diag(lru_wa), block_diag(lru_wx)
    hw = W_B // 2
    w_lru = jnp.stack([jnp.concatenate([wa_d[:, :hw, :hw], wx_d[:, :hw, :hw]], axis=-1),
                       jnp.concatenate([wa_d[:, hw:, hw:], wx_d[:, hw:, hw:]], axis=-1)], axis=1).astype(BF16)
    mem_kv = _mem_kv_call(mem, mem_norm_g, w_mem_kv.astype(BF16))
    w_br_b = w_br.astype(BF16)
    w_out_b = w_out.astype(BF16)

    pos_col = positions[:, :, None]
    pos_row = positions[:, None, :]
    row = lambda a, l: a[l][None, :]
    for l in range(depth):
        params = [
            ("norm_g", row(norm_g, l)), ("w_in", w_in_p), ("conv_a", mlstm_conv_w[l]), ("b_if", b_if[l]),
            ("mlstm_norm_g", row(mlstm_norm_g, l)), ("lru_conv_w", lru_conv_w[l]), ("lru_conv_b", row(lru_conv_b, l)),
            ("w_lru", w_lru), ("lru_ba", row(lru_ba, l)), ("lru_bx", row(lru_bx, l)), ("lru_lambda", row(lru_lambda, l)),
            ("sinks", row(swa_sinks, l)), ("hgrn_lb", hgrn_lb), ("hgrn_norm_g", row(hgrn_norm_g, l)),
            ("w_br", w_br_b), ("w_out", w_out_b), ("final_norm_g", final_norm_g[None, :]),
        ]
        x = _layer_call(x, pos_col, pos_row, mem_kv, params, layer=l, final=(l == depth - 1), tq=tq)
    return x
```

```python
import functools
import math

import jax
import jax.numpy as jnp
from jax import lax
from jax.experimental import pallas as pl
from jax.experimental.pallas import tpu as pltpu

F32 = jnp.float32
BF16 = jnp.bfloat16

EPS = 1e-6
CONV_W = 4
N_HEAD_A = 4
W_A = 512
L_A = 128
INIT_M = -1e30
W_B = 512
LRU_C = 8.0
N_HEAD_C = 8
W_C = 512
KV_W_C = 128
BLK_C = 128
N_HEAD_D = 4
W_D = 512
DIAG_D = 8
W_E = 256
N_BRANCH = 5
BRANCH_W = (W_A, W_B, W_C, W_D, W_E)

LANES = 128
SUBLANES = 8
VMEM_LIMIT_BYTES = 61 * 1024 * 1024

_OFF = {}
_acc = 0
for _name, _w in (("a_qk", 2 * W_A), ("a_v", W_A), ("a_o", W_A), ("a_z", W_A), ("a_if", LANES),
                  ("b_x", W_B), ("b_z", W_B),
                  ("c_q", W_C), ("c_k", KV_W_C), ("c_v", KV_W_C), ("c_z", W_C),
                  ("d_q", W_D), ("d_f", W_D), ("d_i", W_D), ("d_z", W_D),
                  ("e_q", W_E), ("e_z", W_E), ("gates", None)):
    _OFF[_name] = _acc
    if _w is not None:
        _acc += _w
N_IF = 2 * N_HEAD_A


def _bf(x):
    return x.astype(BF16)


def _dot(a, b):
    return jnp.dot(a, b, preferred_element_type=F32)


def _dot_nt(a, b):
    return lax.dot_general(a, b, (((1,), (1,)), ((), ())), preferred_element_type=F32)


def _dot_tn(a, b):
    return lax.dot_general(a, b, (((0,), (0,)), ((), ())), preferred_element_type=F32)


def _silu(x):
    return x * jax.nn.sigmoid(x)


def _log_sigmoid(x):
    return jnp.minimum(x, 0.0) - jnp.log(1.0 + jnp.exp(-jnp.abs(x)))


def _softplus(x):
    return jnp.maximum(x, 0.0) + jnp.log(1.0 + jnp.exp(-jnp.abs(x)))


def _linear_scan_rows(a, b, h0):
    t, w = a.shape
    groups = t // SUBLANES
    a3 = a.reshape(groups, SUBLANES, w)
    b3 = b.reshape(groups, SUBLANES, w)
    sub = lax.broadcasted_iota(jnp.int32, a3.shape, 1)
    s = 1
    while s < SUBLANES:
        keep = sub >= s
        a_s = jnp.where(keep, pltpu.roll(a3, s, 1), 1.0)
        b_s = jnp.where(keep, pltpu.roll(b3, s, 1), 0.0)
        b3 = a3 * b_s + b3
        a3 = a3 * a_s
        s *= 2
    yield
    carry = h0
    out = []
    for g in range(groups):
        hg = b3[g] + a3[g] * carry
        out.append(hg)
        carry = hg[SUBLANES - 1:SUBLANES, :]
        if g % SUBLANES == SUBLANES - 1:
            yield
    return jnp.concatenate(out, axis=0)


def _rmsnorm(x, g):
    return x * lax.rsqrt(jnp.mean(x * x, axis=-1, keepdims=True) + EPS) * g


def _cumsum_rows(tri, x):
    hi = _bf(x)
    r1 = x - hi.astype(F32)
    mid = _bf(r1)
    lo = _bf(r1 - mid.astype(F32))
    return _dot(tri, hi) + _dot(tri, mid) + _dot(tri, lo)


def _block_row_bcast(b, blk, idx):
    n = b.shape[0] // blk
    b3 = b.reshape(n, blk, b.shape[1])
    return jnp.broadcast_to(b3[:, idx:idx + 1, :], b3.shape).reshape(b.shape)


def _causal_conv(buf, x, w_ref, t):
    buf[pl.ds(SUBLANES, t), :] = x
    y = w_ref[0:1, :] * buf[pl.ds(SUBLANES - 3, t), :]
    for k in range(1, CONV_W):
        y = y + w_ref[k:k + 1, :] * buf[pl.ds(SUBLANES - 3 + k, t), :]
    buf[pl.ds(0, SUBLANES), :] = buf[pl.ds(t, SUBLANES), :]
    return y


def _mlstm_head(q, k, v, g, icol, grow, irow, c_st, n_st, m_st, causal):
    L = q.shape[0]
    dm = jnp.where(causal, g - grow + irow, -jnp.inf)
    g_end = g[L - 1:L, :]
    w_end = g_end - g + icol
    m_loc = jnp.max(w_end, axis=0, keepdims=True)
    ke = k * jnp.exp(w_end - m_loc)
    vb = _bf(v)
    c_loc = _dot_tn(_bf(ke), vb)
    n_loc = jnp.sum(ke, axis=0, keepdims=True)

    a_inter = g + m_st
    m_j = jnp.maximum(a_inter, jnp.max(dm, axis=-1, keepdims=True))
    qb = _bf(q)
    p = jnp.exp(dm - m_j) * _dot_nt(qb, _bf(k))
    w_inter = jnp.exp(a_inter - m_j)
    num = _dot(_bf(p), vb) + w_inter * _dot(qb, _bf(c_st))
    den = jnp.sum(p, axis=-1, keepdims=True) + w_inter * jnp.sum(q * n_st, axis=-1, keepdims=True)
    h = num / jnp.maximum(jnp.abs(den), jnp.exp(-m_j))

    m_new = jnp.maximum(g_end + m_st, m_loc)
    a = jnp.exp(g_end + m_st - m_new)
    b = jnp.exp(m_loc - m_new)
    return h, a * c_st + b * c_loc, a * n_st + b * n_loc, m_new


def _hgrn_head(qs, kk, v, b, st_t, ri, ci):
    t = qs.shape[0]
    rows = ri[:, 0:1]
    o = _dot_nt(_bf(qs * jnp.exp(b)), _bf(st_t))
    b_end = b[t - 1:t, :]
    vb = _bf(v)
    st_new = st_t * jnp.exp(b_end) + _dot_tn(vb, _bf(kk * jnp.exp(b_end - b)))
    yield

    amat = None
    s = t // 2
    while s >= DIAG_D:
        sh = (2 * s).bit_length() - 1
        e = jnp.exp(-jnp.abs(b - _block_row_bcast(b, 2 * s, s - 1)))
        upper = (rows & (2 * s - 1)) >= s
        ql = qs * jnp.where(upper, e, 0.0)
        kl = kk * jnp.where(upper, 0.0, e)
        al = _dot_nt(_bf(ql), _bf(kl))
        same = (ri >> sh) == (ci >> sh)
        amat = jnp.where(same, al, 0.0 if amat is None else amat)
        s //= 2
        yield
    sh = DIAG_D.bit_length() - 1
    d = b - _block_row_bcast(b, DIAG_D, 0)
    ad = _dot_nt(_bf(qs * jnp.exp(d)), _bf(kk * jnp.exp(-d)))
    diag = ((ri >> sh) == (ci >> sh)) & (ci <= ri)
    amat = jnp.where(diag, ad, amat)
    return o + _dot(_bf(amat), vb), st_new


def _pair_attention(qp, kmat, vmat, scale, bias_fn, sink_col):
    t = qp.shape[0]
    lane = lax.broadcasted_iota(jnp.int32, qp.shape, 1)
    lo = lane < (LANES // 2)
    assert math.log2(scale).is_integer()
    qs = qp * scale
    q2 = jnp.concatenate([jnp.where(lo, qs, 0.0), jnp.where(lo, 0.0, qs)], axis=0)
    s = bias_fn(_dot_nt(_bf(q2), kmat))
    m = jnp.max(s, axis=-1, keepdims=True)
    if sink_col is not None:
        m = jnp.maximum(m, sink_col)
    p = jnp.exp(s - m)
    den = jnp.sum(p, axis=-1, keepdims=True)
    if sink_col is not None:
        den = den + jnp.exp(sink_col - m)
    o2 = _dot(_bf(p), vmat) / den
    return jnp.where(lo, o2[:t], o2[t:])


def _run_with_fill(tasks, issue):
    for group in tasks:
        active = [(gen, pieces, [0]) for gen, pieces in group]
        while active:
            for task in list(active):
                gen, pieces, count = task
                try:
                    next(gen)
                except StopIteration:
                    active.remove(task)
                    continue
                if count[0] < len(pieces):
                    issue(pieces[count[0]])
                count[0] += 1


PREP_COLS = 384


def _w_in_prep_kernel(wt_ref, o_ref, *, gate_step, gate_rows):
    t = wt_ref[0]
    row = lax.broadcasted_iota(jnp.int32, t.shape, 0)
    keep = (pl.program_id(1) != gate_step) | (row < gate_rows)
    o_ref[...] = _bf(jnp.where(keep, t, 0.0).T)


def _w_in_prep_call(w_in, n_a):
    depth, d_model, n_in = w_in.shape
    n_pad = n_in + LANES - N_IF
    gate_end = n_a + LANES
    assert gate_end % PREP_COLS == 0 and n_pad % PREP_COLS == 0
    gate_step = gate_end // PREP_COLS - 1
    shift = (LANES - N_IF) // SUBLANES

    def window(l, g):
        return l, (g * (PREP_COLS // SUBLANES) - jnp.where(g <= gate_step, 0, shift)) * SUBLANES, 0

    return pl.pallas_call(
        functools.partial(_w_in_prep_kernel, gate_step=gate_step, gate_rows=(n_a + N_IF) % PREP_COLS),
        grid=(depth, n_pad // PREP_COLS),
        in_specs=[pl.BlockSpec((pl.Element(1), pl.Element(PREP_COLS), pl.Element(d_model)), window)],
        out_specs=pl.BlockSpec((None, d_model, PREP_COLS), lambda l, g: (l, 0, g)),
        out_shape=jax.ShapeDtypeStruct((depth, d_model, n_pad), BF16),
        compiler_params=pltpu.CompilerParams(dimension_semantics=("arbitrary", "arbitrary")),
        name="w_in_prep",
    )(jnp.swapaxes(w_in, 1, 2))


def _mem_kv_kernel(mem_ref, g_ref, w_ref, o_ref):
    o_ref[...] = _bf(_dot(_bf(_rmsnorm(mem_ref[0], g_ref[...])), w_ref[...]))


def _mem_kv_call(mem, mem_norm_g, w_kv):
    bsz, m_len, d_model = mem.shape
    depth, _, n_kv = w_kv.shape
    return pl.pallas_call(
        _mem_kv_kernel,
        grid=(depth, bsz),
        in_specs=[pl.BlockSpec((1, m_len, d_model), lambda l, b: (b, 0, 0)),
                  pl.BlockSpec((None, 1, d_model), lambda l, b: (l, 0, 0)),
                  pl.BlockSpec((None, d_model, n_kv), lambda l, b: (l, 0, 0))],
        out_specs=pl.BlockSpec((None, None, m_len, n_kv), lambda l, b: (l, b, 0, 0)),
        out_shape=jax.ShapeDtypeStruct((depth, bsz, m_len, n_kv), BF16),
        compiler_params=pltpu.CompilerParams(dimension_semantics=("arbitrary", "arbitrary")),
        name="mem_kv",
    )(mem, mem_norm_g[:, None, :], w_kv)


def _layer_kernel(x_ref, pc_ref, pr_ref, kv_ref, ng_ref, win_ref, cwa_ref, bif_ref, mng_ref,
                  lcw_ref, lcb_ref, wlru_ref, lba_ref, lbx_ref, lam_ref, sink_ref, hlb_ref, hng_ref,
                  wbr_ref, wout_ref, fg_ref,
                  o_ref,
                  conv_a, conv_b, m_c, m_n, m_m, lru_h, swa_k, swa_v, swa_p, h_st,
                  *, layer, final, tq):
    c = pl.program_id(1)
    d_model = x_ref.shape[-1]

    @pl.when(c == 0)
    def _init():
        conv_a[pl.ds(0, SUBLANES), :] = jnp.zeros((SUBLANES, conv_a.shape[1]), F32)
        conv_b[pl.ds(0, SUBLANES), :] = jnp.zeros((SUBLANES, conv_b.shape[1]), F32)
        m_c[...] = jnp.zeros(m_c.shape, F32)
        m_n[...] = jnp.zeros(m_n.shape, F32)
        m_m[...] = jnp.full(m_m.shape, INIT_M, F32)
        lru_h[...] = jnp.zeros(lru_h.shape, F32)
        swa_k[...] = jnp.zeros(swa_k.shape, BF16)
        swa_v[...] = jnp.zeros(swa_v.shape, BF16)
        swa_p[...] = jnp.zeros(swa_p.shape, jnp.int32)
        h_st[...] = jnp.zeros(h_st.shape, F32)

    x = x_ref[0]
    hb = _bf(_rmsnorm(x, ng_ref[...]))

    gate_piece = d_model // 2
    n_gate_pieces = d_model // gate_piece
    widths = {"a_o": W_A, "a_z": W_A, "b_x": W_B, "b_z": W_B, "c_q": W_C, "c_kv": 2 * KV_W_C, "c_z": W_C,
              "d_q": W_D, "d_f": W_D, "d_i": W_D, "d_z": W_D, "e_q": W_E, "e_z": W_E}
    offsets = dict(_OFF, c_kv=_OFF["c_k"])
    gates = [("gate", j, hf) for j in range(N_BRANCH) for hf in range(n_gate_pieces)]
    pending = ["a_o", "a_z", "b_x", "b_z", "c_q", "c_kv"] + gates + ["c_z", "d_f", "d_q", "d_i", "d_z", "e_q", "e_z"]
    issued = {}

    def issue(piece):
        if isinstance(piece, tuple):
            _, j, hf = piece
            g0 = _OFF["gates"] + j * d_model + hf * gate_piece
            issued[piece] = jax.nn.sigmoid(_dot(hb, win_ref[:, g0:g0 + gate_piece]))
        else:
            o = offsets[piece]
            issued[piece] = _dot(hb, win_ref[:, o:o + widths[piece]])

    def issue_pending(n):
        for _ in range(min(n, len(pending))):
            issue(pending.pop(0))

    def proj(name, width):
        if name in widths:
            assert widths[name] == width
            if name not in issued:
                pending.remove(name)
                issue(name)
            return issued[name]
        o = _OFF[name]
        return _dot(hb, win_ref[:, o:o + width])

    branch_out = {}
    hd = LANES // 2

    def mixer_a():
        qk = _silu(_causal_conv(conv_a, proj("a_qk", 2 * W_A), cwa_ref, tq))
        yield
        v_a = proj("a_v", W_A)
        gif = proj("a_if", LANES) + bif_ref[...]
        logf = _log_sigmoid(gif)
        ri_a = lax.broadcasted_iota(jnp.int32, (L_A, L_A), 0)
        ci_a = lax.broadcasted_iota(jnp.int32, (L_A, L_A), 1)
        causal_a = ci_a <= ri_a
        tri_a = jnp.where(causal_a, 1.0, 0.0).astype(BF16)
        hm_chunks = []
        for n in range(tq // L_A):
            rs = slice(n * L_A, (n + 1) * L_A)
            gif_c = gif[rs]
            g_all = _cumsum_rows(tri_a, logf[rs])
            g_t = g_all.T
            gif_t = gif_c.T
            heads = []
            for h in range(N_HEAD_A):
                ls = slice(h * LANES, (h + 1) * LANES)
                hh, c_new, n_new, m_new = _mlstm_head(
                    qk[rs, ls] * (LANES ** -0.5), qk[rs, W_A + h * LANES:W_A + (h + 1) * LANES], v_a[rs, ls],
                    g_all[:, N_HEAD_A + h:N_HEAD_A + h + 1], gif_c[:, h:h + 1],
                    g_t[N_HEAD_A + h:N_HEAD_A + h + 1, :], gif_t[h:h + 1, :],
                    m_c[h], m_n[h, 0:1, :], m_m[h, 0:1, 0:1], causal_a)
                m_c[h] = c_new
                m_n[h] = jnp.broadcast_to(n_new, m_n.shape[1:])
                m_m[h] = jnp.broadcast_to(m_new, m_m.shape[1:])
                hh = hh * lax.rsqrt(jnp.mean(hh * hh, axis=-1, keepdims=True) + EPS)
                heads.append(hh)
                yield
            hm_chunks.append(jnp.concatenate(heads, axis=1))
        hm = jnp.concatenate(hm_chunks, axis=0) if len(hm_chunks) > 1 else hm_chunks[0]
        branch_out["a"] = hm * mng_ref[...] * jax.nn.sigmoid(proj("a_o", W_A)) * _silu(proj("a_z", W_A))

    def mixer_b():
        xc = _causal_conv(conv_b, proj("b_x", W_B), lcw_ref, tq) + lcb_ref[...]
        yield
        xcb = _bf(xc)
        half = W_B // 2
        rx0 = _dot(xcb[:, :half], wlru_ref[0])
        rx1 = _dot(xcb[:, half:], wlru_ref[1])
        r = jax.nn.sigmoid(jnp.concatenate([rx0[:, :half], rx1[:, :half]], axis=1) + lba_ref[...])
        i_g = jax.nn.sigmoid(jnp.concatenate([rx0[:, half:], rx1[:, half:]], axis=1) + lbx_ref[...])
        yield
        a_t = jnp.exp(-LRU_C * r * _softplus(-lam_ref[...]))
        b_t = jnp.sqrt(1.0 - a_t * a_t) * (i_g * xc)
        yield
        h_lru = yield from _linear_scan_rows(a_t, b_t, lru_h[0:1, :])
        lru_h[...] = jnp.broadcast_to(h_lru[tq - 1:tq, :], lru_h.shape)
        branch_out["b"] = h_lru * _silu(proj("b_z", W_B))

    def mixer_c():
        q_c = proj("c_q", W_C)
        kv_c = _bf(proj("c_kv", 2 * KV_W_C))
        k_c = kv_c[:, :KV_W_C]
        v_c = kv_c[:, KV_W_C:]
        pos_c = pc_ref[0]
        pos_r = pr_ref[0]
        rows2 = lax.broadcasted_iota(jnp.int32, (2 * BLK_C, 1), 0)
        ri_c = lax.broadcasted_iota(jnp.int32, (BLK_C, 2 * BLK_C), 0)
        ci_c = lax.broadcasted_iota(jnp.int32, (BLK_C, 2 * BLK_C), 1)
        rel = ri_c + BLK_C - ci_c
        in_window = (rel >= 0) & (rel < BLK_C)
        yc_blocks = []
        for n in range(tq // BLK_C):
            rs = slice(n * BLK_C, (n + 1) * BLK_C)
            k_prev = swa_k[...] if n == 0 else k_c[(n - 1) * BLK_C:n * BLK_C]
            v_prev = swa_v[...] if n == 0 else v_c[(n - 1) * BLK_C:n * BLK_C]
            p_prev = swa_p[0:1, :] if n == 0 else pos_r[:, (n - 1) * BLK_C:n * BLK_C]
            kk = jnp.concatenate([k_prev, k_c[rs]], axis=0)
            vv = jnp.concatenate([v_prev, v_c[rs]], axis=0)
            pk = jnp.concatenate([p_prev, pos_r[:, rs]], axis=1)
            dist = jnp.abs(pos_c[rs] - pk).astype(F32)
            first_key = jnp.where((c * (tq // BLK_C) + n) == 0, BLK_C, 0)
            valid = in_window & (ci_c >= first_key)
            dist2 = jnp.concatenate([dist, dist], axis=0)
            valid2 = jnp.concatenate([valid, valid], axis=0)
            kdup = [jnp.concatenate([kk[:, j * hd:(j + 1) * hd]] * 2, axis=1) for j in range(2)]
            vdup = [jnp.concatenate([vv[:, j * hd:(j + 1) * hd]] * 2, axis=1) for j in range(2)]
            pairs = []
            for p in range(N_HEAD_C // 2):
                kvh = (2 * p) // (N_HEAD_C // 2)
                top = rows2 < BLK_C
                slope = jnp.where(top, 2.0 ** (-(2 * p + 1)), 2.0 ** (-(2 * p + 2)))
                sink = jnp.where(top, sink_ref[0, 2 * p], sink_ref[0, 2 * p + 1])

                def bias(sc, slope=slope, dist2=dist2, valid2=valid2):
                    return jnp.where(valid2, sc - slope * dist2, -jnp.inf)

                pairs.append(_pair_attention(q_c[rs, p * LANES:(p + 1) * LANES], kdup[kvh], vdup[kvh],
                                             hd ** -0.5, bias, sink))
                yield
            yc_blocks.append(jnp.concatenate(pairs, axis=1))
        last = slice(tq - BLK_C, tq)
        swa_k[...] = k_c[last]
        swa_v[...] = v_c[last]
        swa_p[...] = jnp.broadcast_to(pos_r[:, last], swa_p.shape)
        yc = jnp.concatenate(yc_blocks, axis=0) if len(yc_blocks) > 1 else yc_blocks[0]
        branch_out["c"] = yc * _silu(proj("c_z", W_C))

    def mixer_d():
        lbp = hlb_ref[...]
        e_lb = jnp.exp(lbp - jnp.max(lbp, axis=0, keepdims=True))
        p_lb = e_lb / jnp.sum(e_lb, axis=0, keepdims=True)
        lb = jnp.zeros((1, W_D), F32)
        for j in range(1, layer + 1):
            lb = lb + p_lb[j:j + 1, :]
        ls_f = jnp.log1p(-lb) + _log_sigmoid(proj("d_f", W_D))
        log_lb = jnp.log(lb)
        mx = jnp.maximum(log_lb, ls_f)
        log_f = mx + jnp.log(1.0 + jnp.exp(-jnp.abs(log_lb - ls_f)))
        k_d = 1.0 - jnp.exp(log_f)
        yield
        ri_d = lax.broadcasted_iota(jnp.int32, (tq, tq), 0)
        ci_d = lax.broadcasted_iota(jnp.int32, (tq, tq), 1)
        tri_d = jnp.where(ci_d <= ri_d, 1.0, 0.0).astype(BF16)
        b_all = _cumsum_rows(tri_d, log_f)
        qs_d = _silu(proj("d_q", W_D)) * (LANES ** -0.5)
        v_d = proj("d_i", W_D)
        yield
        hd_heads = []
        for h in range(N_HEAD_D):
            ls = slice(h * LANES, (h + 1) * LANES)
            o_h, st_new = yield from _hgrn_head(qs_d[:, ls], k_d[:, ls], v_d[:, ls], b_all[:, ls], h_st[h],
                                                ri_d, ci_d)
            h_st[h] = st_new
            hd_heads.append(o_h * lax.rsqrt(jnp.mean(o_h * o_h, axis=-1, keepdims=True) + EPS))
            yield
        branch_out["d"] = jnp.concatenate(hd_heads, axis=1) * hng_ref[...] * _silu(proj("d_z", W_D))

    def mixer_e():
        q_e = proj("e_q", W_E)
        pairs = []
        for p in range(W_E // LANES):
            ls = slice(p * LANES, (p + 1) * LANES)
            pairs.append(_pair_attention(q_e[:, ls], kv_ref[:, ls], kv_ref[:, W_E + p * LANES:W_E + (p + 1) * LANES],
                                         hd ** -0.5, lambda sc: sc, None))
            yield
        branch_out["e"] = jnp.concatenate(pairs, axis=1) * _silu(proj("e_z", W_E))

    _run_with_fill([[(mixer_a(), [2] * (N_HEAD_A * (tq // L_A)))],
                    [(mixer_b(), [2, 0, 2])],
                    [(mixer_c(), [1] * (N_HEAD_C // 2 * (tq // BLK_C)))],
                    [(mixer_d(), [])],
                    [(mixer_e(), [])]], issue_pending)

    issue_pending(len(pending))
    y_pieces = []
    for hf in range(n_gate_pieces):
        cs = slice(hf * gate_piece, (hf + 1) * gate_piece)
        y = None
        row0 = 0
        for j, (name, wj) in enumerate(zip("abcde", BRANCH_W)):
            t_j = issued[("gate", j, hf)] * _dot(_bf(branch_out[name]), wbr_ref[row0:row0 + wj, cs])
            y = t_j if y is None else y + t_j
            row0 += wj
        y_pieces.append(_bf(y))
    x_new = x + _dot(jnp.concatenate(y_pieces, axis=1), wout_ref[...])
    if final:
        x_new = _rmsnorm(x_new, fg_ref[...])
    o_ref[0] = x_new


def _const_spec(shape):
    nd = len(shape)
    return pl.BlockSpec(shape, lambda b, c: (0,) * nd, pipeline_mode=pl.Buffered(1))


_STACKED = ("w_in", "w_lru", "w_br", "w_out")


def _layer_spec(shape, layer):
    nd = len(shape)
    return pl.BlockSpec((None,) + tuple(shape[1:]), lambda b, c: (layer,) + (0,) * (nd - 1),
                        pipeline_mode=pl.Buffered(1))


def _layer_call(x, pos_col, pos_row, mem_kv, params, *, layer, final, tq):
    bsz, seq, d_model = x.shape
    in_specs = [
        pl.BlockSpec((1, tq, d_model), lambda b, c: (b, c, 0)),
        pl.BlockSpec((1, tq, 1), lambda b, c: (b, c, 0)),
        pl.BlockSpec((1, 1, tq), lambda b, c: (b, 0, c)),
        pl.BlockSpec((None, None) + tuple(mem_kv.shape[2:]), lambda b, c: (layer, b, 0, 0)),
    ]
    for name, p in params:
        if name == "sinks":
            in_specs.append(pl.BlockSpec(memory_space=pltpu.SMEM))
        elif name in _STACKED:
            in_specs.append(_layer_spec(p.shape, layer))
        else:
            in_specs.append(_const_spec(p.shape))
    scratch = [
        pltpu.VMEM((tq + SUBLANES, 2 * W_A), F32),
        pltpu.VMEM((tq + SUBLANES, W_B), F32),
        pltpu.VMEM((N_HEAD_A, LANES, LANES), F32),
        pltpu.VMEM((N_HEAD_A, SUBLANES, LANES), F32),
        pltpu.VMEM((N_HEAD_A, SUBLANES, LANES), F32),
        pltpu.VMEM((SUBLANES, W_B), F32),
        pltpu.VMEM((BLK_C, KV_W_C), BF16),
        pltpu.VMEM((BLK_C, KV_W_C), BF16),
        pltpu.VMEM((SUBLANES, BLK_C), jnp.int32),
        pltpu.VMEM((N_HEAD_D, LANES, LANES), F32),
    ]
    return pl.pallas_call(
        functools.partial(_layer_kernel, layer=layer, final=final, tq=tq),
        grid=(bsz, seq // tq),
        in_specs=in_specs,
        out_specs=pl.BlockSpec((1, tq, d_model), lambda b, c: (b, c, 0)),
        out_shape=jax.ShapeDtypeStruct(x.shape, x.dtype),
        scratch_shapes=scratch,
        compiler_params=pltpu.CompilerParams(dimension_semantics=("arbitrary", "arbitrary"),
                                             vmem_limit_bytes=VMEM_LIMIT_BYTES),
        name=f"hybrid_layer_{layer}",
    )(x, pos_col, pos_row, mem_kv, *[p for _, p in params])


def _tile_rows(seq):
    for t in (256, 128):
        if seq % t == 0:
            return t
    raise ValueError("sequence length must be a multiple of 128")


def kernel(x, mem, positions, norm_g, w_in, mlstm_conv_w, mlstm_b_if, mlstm_norm_g, lru_conv_w, lru_conv_b, lru_wa, lru_ba, lru_wx, lru_bx, lru_lambda, swa_sinks, hgrn_lb, hgrn_norm_g, mem_norm_g, w_mem_kv, w_br, w_out, final_norm_g):
    depth, d_model, n_in = w_in.shape
    n_a = 5 * W_A
    assert n_in == n_a + N_IF + 2 * W_B + 2 * W_C + 2 * KV_W_C + 4 * W_D + 2 * W_E + N_BRANCH * d_model
    tq = _tile_rows(x.shape[1])

    w_in_p = _w_in_prep_call(w_in, n_a)
    b_if = jnp.pad(mlstm_b_if, ((0, 0), (0, LANES - N_IF)))[:, None, :]
    nb, bd = lru_wa.shape[1], lru_wa.shape[2]
    eye = jnp.eye(nb, dtype=lru_wa.dtype)

    def block_diag(w):
        return (eye[None, :, None, :, None] * w[:, :, :, None, :]).reshape(depth, nb * bd, nb * bd)

    wa_d, wx_d = block_diag(lru_wa), block_diag(lru_wx)
    hw = W_B // 2
    w_lru = jnp.stack([jnp.concatenate([wa_d[:, :hw, :hw], wx_d[:, :hw, :hw]], axis=-1),
                       jnp.concatenate([wa_d[:, hw:, hw:], wx_d[:, hw:, hw:]], axis=-1)], axis=1).astype(BF16)
    mem_kv = _mem_kv_call(mem, mem_norm_g, w_mem_kv.astype(BF16))
    w_br_b = w_br.astype(BF16)
    w_out_b = w_out.astype(BF16)

    pos_col = positions[:, :, None]
    pos_row = positions[:, None, :]
    row = lambda a, l: a[l][None, :]
    for l in range(depth):
        params = [
            ("norm_g", row(norm_g, l)), ("w_in", w_in_p), ("conv_a", mlstm_conv_w[l]), ("b_if", b_if[l]),
            ("mlstm_norm_g", row(mlstm_norm_g, l)), ("lru_conv_w", lru_conv_w[l]), ("lru_conv_b", row(lru_conv_b, l)),
            ("w_lru", w_lru), ("lru_ba", row(lru_ba, l)), ("lru_bx", row(lru_bx, l)), ("lru_lambda", row(lru_lambda, l)),
            ("sinks", row(swa_sinks, l)), ("hgrn_lb", hgrn_lb), ("hgrn_norm_g", row(hgrn_norm_g, l)),
            ("w_br", w_br_b), ("w_out", w_out_b), ("final_norm_g", final_norm_g[None, :]),
        ]
        x = _layer_call(x, pos_col, pos_row, mem_kv, params, layer=l, final=(l == depth - 1), tq=tq)
    return x
```

```python
import functools
import math

import jax
import jax.numpy as jnp
from jax import lax
from jax.experimental import pallas as pl
from jax.experimental.pallas import tpu as pltpu

F32 = jnp.float32
BF16 = jnp.bfloat16

EPS = 1e-6
CONV_W = 4
N_HEAD_A = 4
W_A = 512
L_A = 128
INIT_M = -1e30
W_B = 512
LRU_C = 8.0
N_HEAD_C = 8
W_C = 512
KV_W_C = 128
BLK_C = 128
N_HEAD_D = 4
W_D = 512
DIAG_D = 8
W_E = 256
N_BRANCH = 5
BRANCH_W = (W_A, W_B, W_C, W_D, W_E)

LANES = 128
SUBLANES = 8
VMEM_LIMIT_BYTES = 58 * 1024 * 1024

_OFF = {}
_acc = 0
for _name, _w in (("a_qk", 2 * W_A), ("a_v", W_A), ("a_o", W_A), ("a_z", W_A), ("a_if", LANES),
                  ("b_x", W_B), ("b_z", W_B),
                  ("c_q", W_C), ("c_k", KV_W_C), ("c_v", KV_W_C), ("c_z", W_C),
                  ("d_q", W_D), ("d_f", W_D), ("d_i", W_D), ("d_z", W_D),
                  ("e_q", W_E), ("e_z", W_E), ("gates", None)):
    _OFF[_name] = _acc
    if _w is not None:
        _acc += _w
N_IF = 2 * N_HEAD_A


def _bf(x):
    return x.astype(BF16)


def _dot(a, b):
    return jnp.dot(a, b, preferred_element_type=F32)


def _dot_nt(a, b):
    return lax.dot_general(a, b, (((1,), (1,)), ((), ())), preferred_element_type=F32)


def _dot_tn(a, b):
    return lax.dot_general(a, b, (((0,), (0,)), ((), ())), preferred_element_type=F32)


def _silu(x):
    return x * jax.nn.sigmoid(x)


def _log_sigmoid(x):
    return jnp.minimum(x, 0.0) - jnp.log(1.0 + jnp.exp(-jnp.abs(x)))


def _softplus(x):
    return jnp.maximum(x, 0.0) + jnp.log(1.0 + jnp.exp(-jnp.abs(x)))


def _linear_scan_rows(a, b, h0):
    t, w = a.shape
    groups = t // SUBLANES
    a3 = a.reshape(groups, SUBLANES, w)
    b3 = b.reshape(groups, SUBLANES, w)
    sub = lax.broadcasted_iota(jnp.int32, a3.shape, 1)
    s = 1
    while s < SUBLANES:
        keep = sub >= s
        a_s = jnp.where(keep, pltpu.roll(a3, s, 1), 1.0)
        b_s = jnp.where(keep, pltpu.roll(b3, s, 1), 0.0)
        b3 = a3 * b_s + b3
        a3 = a3 * a_s
        s *= 2
    yield
    carry = h0
    out = []
    for g in range(groups):
        hg = b3[g] + a3[g] * carry
        out.append(hg)
        carry = hg[SUBLANES - 1:SUBLANES, :]
        if g % SUBLANES == SUBLANES - 1:
            yield
    return jnp.concatenate(out, axis=0)


def _rmsnorm(x, g):
    return x * lax.rsqrt(jnp.mean(x * x, axis=-1, keepdims=True) + EPS) * g


def _cumsum_rows(tri, x):
    hi = _bf(x)
    r1 = x - hi.astype(F32)
    mid = _bf(r1)
    lo = _bf(r1 - mid.astype(F32))
    return _dot(tri, hi) + _dot(tri, mid) + _dot(tri, lo)


def _block_row_bcast(b, blk, idx):
    n = b.shape[0] // blk
    b3 = b.reshape(n, blk, b.shape[1])
    return jnp.broadcast_to(b3[:, idx:idx + 1, :], b3.shape).reshape(b.shape)


def _causal_conv(buf, x, w_ref, t):
    buf[pl.ds(SUBLANES, t), :] = x
    y = w_ref[0:1, :] * buf[pl.ds(SUBLANES - 3, t), :]
    for k in range(1, CONV_W):
        y = y + w_ref[k:k + 1, :] * buf[pl.ds(SUBLANES - 3 + k, t), :]
    buf[pl.ds(0, SUBLANES), :] = buf[pl.ds(t, SUBLANES), :]
    return y


def _mlstm_head(q, k, v, g, icol, grow, irow, c_st, n_st, m_st, causal):
    L = q.shape[0]
    dm = jnp.where(causal, g - grow + irow, -jnp.inf)
    g_end = g[L - 1:L, :]
    w_end = g_end - g + icol
    m_loc = jnp.max(w_end, axis=0, keepdims=True)
    ke = k * jnp.exp(w_end - m_loc)
    vb = _bf(v)
    c_loc = _dot_tn(_bf(ke), vb)
    n_loc = jnp.sum(ke, axis=0, keepdims=True)

    a_inter = g + m_st
    m_j = jnp.maximum(a_inter, jnp.max(dm, axis=-1, keepdims=True))
    qb = _bf(q)
    p = jnp.exp(dm - m_j) * _dot_nt(qb, _bf(k))
    w_inter = jnp.exp(a_inter - m_j)
    num = _dot(_bf(p), vb) + w_inter * _dot(qb, _bf(c_st))
    den = jnp.sum(p, axis=-1, keepdims=True) + w_inter * jnp.sum(q * n_st, axis=-1, keepdims=True)
    h = num / jnp.maximum(jnp.abs(den), jnp.exp(-m_j))

    m_new = jnp.maximum(g_end + m_st, m_loc)
    a = jnp.exp(g_end + m_st - m_new)
    b = jnp.exp(m_loc - m_new)
    return h, a * c_st + b * c_loc, a * n_st + b * n_loc, m_new


def _hgrn_head(qs, kk, v, b, st_t, ri, ci):
    t = qs.shape[0]
    rows = ri[:, 0:1]
    o = _dot_nt(_bf(qs * jnp.exp(b)), _bf(st_t))
    b_end = b[t - 1:t, :]
    vb = _bf(v)
    st_new = st_t * jnp.exp(b_end) + _dot_tn(vb, _bf(kk * jnp.exp(b_end - b)))
    yield

    amat = None
    s = t // 2
    while s >= DIAG_D:
        sh = (2 * s).bit_length() - 1
        e = jnp.exp(-jnp.abs(b - _block_row_bcast(b, 2 * s, s - 1)))
        upper = (rows & (2 * s - 1)) >= s
        ql = qs * jnp.where(upper, e, 0.0)
        kl = kk * jnp.where(upper, 0.0, e)
        al = _dot_nt(_bf(ql), _bf(kl))
        same = (ri >> sh) == (ci >> sh)
        amat = jnp.where(same, al, 0.0 if amat is None else amat)
        s //= 2
        yield
    sh = DIAG_D.bit_length() - 1
    d = b - _block_row_bcast(b, DIAG_D, 0)
    ad = _dot_nt(_bf(qs * jnp.exp(d)), _bf(kk * jnp.exp(-d)))
    diag = ((ri >> sh) == (ci >> sh)) & (ci <= ri)
    amat = jnp.where(diag, ad, amat)
    return o + _dot(_bf(amat), vb), st_new


def _pair_attention(qp, kmat, vmat, scale, bias_fn, sink_col):
    t = qp.shape[0]
    lane = lax.broadcasted_iota(jnp.int32, qp.shape, 1)
    lo = lane < (LANES // 2)
    assert math.log2(scale).is_integer()
    qs = qp * scale
    q2 = jnp.concatenate([jnp.where(lo, qs, 0.0), jnp.where(lo, 0.0, qs)], axis=0)
    s = bias_fn(_dot_nt(_bf(q2), kmat))
    m = jnp.max(s, axis=-1, keepdims=True)
    if sink_col is not None:
        m = jnp.maximum(m, sink_col)
    p = jnp.exp(s - m)
    den = jnp.sum(p, axis=-1, keepdims=True)
    if sink_col is not None:
        den = den + jnp.exp(sink_col - m)
    o2 = _dot(_bf(p), vmat) / den
    return jnp.where(lo, o2[:t], o2[t:])


def _run_with_fill(tasks, issue):
    for group in tasks:
        active = [(gen, pieces, [0]) for gen, pieces in group]
        while active:
            for task in list(active):
                gen, pieces, count = task
                try:
                    next(gen)
                except StopIteration:
                    active.remove(task)
                    continue
                if count[0] < len(pieces):
                    issue(pieces[count[0]])
                count[0] += 1


PREP_COLS = 384


def _w_in_prep_kernel(wt_ref, o_ref, *, gate_step, gate_rows):
    t = wt_ref[0]
    row = lax.broadcasted_iota(jnp.int32, t.shape, 0)
    keep = (pl.program_id(1) != gate_step) | (row < gate_rows)
    o_ref[...] = _bf(jnp.where(keep, t, 0.0).T)


def _w_in_prep_call(w_in, n_a):
    depth, d_model, n_in = w_in.shape
    n_pad = n_in + LANES - N_IF
    gate_end = n_a + LANES
    assert gate_end % PREP_COLS == 0 and n_pad % PREP_COLS == 0
    gate_step = gate_end // PREP_COLS - 1
    shift = (LANES - N_IF) // SUBLANES

    def window(l, g):
        return l, (g * (PREP_COLS // SUBLANES) - jnp.where(g <= gate_step, 0, shift)) * SUBLANES, 0

    return pl.pallas_call(
        functools.partial(_w_in_prep_kernel, gate_step=gate_step, gate_rows=(n_a + N_IF) % PREP_COLS),
        grid=(depth, n_pad // PREP_COLS),
        in_specs=[pl.BlockSpec((pl.Element(1), pl.Element(PREP_COLS), pl.Element(d_model)), window)],
        out_specs=pl.BlockSpec((None, d_model, PREP_COLS), lambda l, g: (l, 0, g)),
        out_shape=jax.ShapeDtypeStruct((depth, d_model, n_pad), BF16),
        compiler_params=pltpu.CompilerParams(dimension_semantics=("arbitrary", "arbitrary")),
        name="w_in_prep",
    )(jnp.swapaxes(w_in, 1, 2))


def _mem_kv_kernel(mem_ref, g_ref, w_ref, o_ref):
    o_ref[...] = _bf(_dot(_bf(_rmsnorm(mem_ref[0], g_ref[...])), w_ref[...]))


def _mem_kv_call(mem, mem_norm_g, w_kv):
    bsz, m_len, d_model = mem.shape
    depth, _, n_kv = w_kv.shape
    return pl.pallas_call(
        _mem_kv_kernel,
        grid=(depth, bsz),
        in_specs=[pl.BlockSpec((1, m_len, d_model), lambda l, b: (b, 0, 0)),
                  pl.BlockSpec((None, 1, d_model), lambda l, b: (l, 0, 0)),
                  pl.BlockSpec((None, d_model, n_kv), lambda l, b: (l, 0, 0))],
        out_specs=pl.BlockSpec((None, None, m_len, n_kv), lambda l, b: (l, b, 0, 0)),
        out_shape=jax.ShapeDtypeStruct((depth, bsz, m_len, n_kv), BF16),
        compiler_params=pltpu.CompilerParams(dimension_semantics=("arbitrary", "arbitrary")),
        name="mem_kv",
    )(mem, mem_norm_g[:, None, :], w_kv)


def _layer_kernel(x_ref, pc_ref, pr_ref, kv_ref, ng_ref, win_ref, cwa_ref, bif_ref, mng_ref,
                  lcw_ref, lcb_ref, wlru_ref, lba_ref, lbx_ref, lam_ref, sink_ref, hlb_ref, hng_ref,
                  wbr_ref, wout_ref, fg_ref,
                  o_ref,
                  conv_a, conv_b, m_c, m_n, m_m, lru_h, swa_k, swa_v, swa_p, h_st,
                  *, layer, final, tq):
    c = pl.program_id(1)
    d_model = x_ref.shape[-1]

    @pl.when(c == 0)
    def _init():
        conv_a[pl.ds(0, SUBLANES), :] = jnp.zeros((SUBLANES, conv_a.shape[1]), F32)
        conv_b[pl.ds(0, SUBLANES), :] = jnp.zeros((SUBLANES, conv_b.shape[1]), F32)
        m_c[...] = jnp.zeros(m_c.shape, F32)
        m_n[...] = jnp.zeros(m_n.shape, F32)
        m_m[...] = jnp.full(m_m.shape, INIT_M, F32)
        lru_h[...] = jnp.zeros(lru_h.shape, F32)
        swa_k[...] = jnp.zeros(swa_k.shape, BF16)
        swa_v[...] = jnp.zeros(swa_v.shape, BF16)
        swa_p[...] = jnp.zeros(swa_p.shape, jnp.int32)
        h_st[...] = jnp.zeros(h_st.shape, F32)

    x = x_ref[0]
    hb = _bf(_rmsnorm(x, ng_ref[...]))

    gate_piece = d_model // 2
    n_gate_pieces = d_model // gate_piece
    widths = {"a_o": W_A, "a_z": W_A, "b_x": W_B, "b_z": W_B, "c_q": W_C, "c_kv": 2 * KV_W_C, "c_z": W_C,
              "d_q": W_D, "d_f": W_D, "d_i": W_D, "d_z": W_D, "e_q": W_E, "e_z": W_E}
    offsets = dict(_OFF, c_kv=_OFF["c_k"])
    gates = [("gate", j, hf) for j in range(N_BRANCH) for hf in range(n_gate_pieces)]
    pending = (["a_o", "a_z", "b_x", "b_z", "c_q", "c_kv"] + gates[:2] + ["c_z", "d_f", "d_q", "d_i"]
               + ["d_z", "e_q", "e_z"] + gates[2:])
    issued = {}

    def issue(piece):
        if isinstance(piece, tuple):
            _, j, hf = piece
            g0 = _OFF["gates"] + j * d_model + hf * gate_piece
            issued[piece] = jax.nn.sigmoid(_dot(hb, win_ref[:, g0:g0 + gate_piece]))
        else:
            o = offsets[piece]
            issued[piece] = _dot(hb, win_ref[:, o:o + widths[piece]])

    def issue_pending(n):
        for _ in range(min(n, len(pending))):
            issue(pending.pop(0))

    def proj(name, width):
        if name in widths:
            assert widths[name] == width
            if name not in issued:
                pending.remove(name)
                issue(name)
            return issued[name]
        o = _OFF[name]
        return _dot(hb, win_ref[:, o:o + width])

    branch_out = {}
    hd = LANES // 2

    def mixer_a():
        qk = _silu(_causal_conv(conv_a, proj("a_qk", 2 * W_A), cwa_ref, tq))
        yield
        v_a = proj("a_v", W_A)
        gif = proj("a_if", LANES) + bif_ref[...]
        logf = _log_sigmoid(gif)
        ri_a = lax.broadcasted_iota(jnp.int32, (L_A, L_A), 0)
        ci_a = lax.broadcasted_iota(jnp.int32, (L_A, L_A), 1)
        causal_a = ci_a <= ri_a
        tri_a = jnp.where(causal_a, 1.0, 0.0).astype(BF16)
        hm_chunks = []
        for n in range(tq // L_A):
            rs = slice(n * L_A, (n + 1) * L_A)
            gif_c = gif[rs]
            g_all = _cumsum_rows(tri_a, logf[rs])
            g_t = g_all.T
            gif_t = gif_c.T
            heads = []
            for h in range(N_HEAD_A):
                ls = slice(h * LANES, (h + 1) * LANES)
                hh, c_new, n_new, m_new = _mlstm_head(
                    qk[rs, ls] * (LANES ** -0.5), qk[rs, W_A + h * LANES:W_A + (h + 1) * LANES], v_a[rs, ls],
                    g_all[:, N_HEAD_A + h:N_HEAD_A + h + 1], gif_c[:, h:h + 1],
                    g_t[N_HEAD_A + h:N_HEAD_A + h + 1, :], gif_t[h:h + 1, :],
                    m_c[h], m_n[h, 0:1, :], m_m[h, 0:1, 0:1], causal_a)
                m_c[h] = c_new
                m_n[h] = jnp.broadcast_to(n_new, m_n.shape[1:])
                m_m[h] = jnp.broadcast_to(m_new, m_m.shape[1:])
                hh = hh * lax.rsqrt(jnp.mean(hh * hh, axis=-1, keepdims=True) + EPS)
                heads.append(hh)
                yield
            hm_chunks.append(jnp.concatenate(heads, axis=1))
        hm = jnp.concatenate(hm_chunks, axis=0) if len(hm_chunks) > 1 else hm_chunks[0]
        branch_out["a"] = hm * mng_ref[...] * jax.nn.sigmoid(proj("a_o", W_A)) * _silu(proj("a_z", W_A))

    def mixer_b():
        xc = _causal_conv(conv_b, proj("b_x", W_B), lcw_ref, tq) + lcb_ref[...]
        yield
        xcb = _bf(xc)
        half = W_B // 2
        rx0 = _dot(xcb[:, :half], wlru_ref[0])
        rx1 = _dot(xcb[:, half:], wlru_ref[1])
        r = jax.nn.sigmoid(jnp.concatenate([rx0[:, :half], rx1[:, :half]], axis=1) + lba_ref[...])
        i_g = jax.nn.sigmoid(jnp.concatenate([rx0[:, half:], rx1[:, half:]], axis=1) + lbx_ref[...])
        yield
        a_t = jnp.exp(-LRU_C * r * _softplus(-lam_ref[...]))
        b_t = jnp.sqrt(1.0 - a_t * a_t) * (i_g * xc)
        yield
        h_lru = yield from _linear_scan_rows(a_t, b_t, lru_h[0:1, :])
        lru_h[...] = jnp.broadcast_to(h_lru[tq - 1:tq, :], lru_h.shape)
        branch_out["b"] = h_lru * _silu(proj("b_z", W_B))

    def mixer_c():
        q_c = proj("c_q", W_C)
        kv_c = _bf(proj("c_kv", 2 * KV_W_C))
        k_c = kv_c[:, :KV_W_C]
        v_c = kv_c[:, KV_W_C:]
        pos_c = pc_ref[0]
        pos_r = pr_ref[0]
        rows2 = lax.broadcasted_iota(jnp.int32, (2 * BLK_C, 1), 0)
        ri_c = lax.broadcasted_iota(jnp.int32, (BLK_C, 2 * BLK_C), 0)
        ci_c = lax.broadcasted_iota(jnp.int32, (BLK_C, 2 * BLK_C), 1)
        rel = ri_c + BLK_C - ci_c
        in_window = (rel >= 0) & (rel < BLK_C)
        yc_blocks = []
        for n in range(tq // BLK_C):
            rs = slice(n * BLK_C, (n + 1) * BLK_C)
            k_prev = swa_k[...] if n == 0 else k_c[(n - 1) * BLK_C:n * BLK_C]
            v_prev = swa_v[...] if n == 0 else v_c[(n - 1) * BLK_C:n * BLK_C]
            p_prev = swa_p[0:1, :] if n == 0 else pos_r[:, (n - 1) * BLK_C:n * BLK_C]
            kk = jnp.concatenate([k_prev, k_c[rs]], axis=0)
            vv = jnp.concatenate([v_prev, v_c[rs]], axis=0)
            pk = jnp.concatenate([p_prev, pos_r[:, rs]], axis=1)
            dist = jnp.abs(pos_c[rs] - pk).astype(F32)
            first_key = jnp.where((c * (tq // BLK_C) + n) == 0, BLK_C, 0)
            valid = in_window & (ci_c >= first_key)
            dist2 = jnp.concatenate([dist, dist], axis=0)
            valid2 = jnp.concatenate([valid, valid], axis=0)
            kdup = [jnp.concatenate([kk[:, j * hd:(j + 1) * hd]] * 2, axis=1) for j in range(2)]
            vdup = [jnp.concatenate([vv[:, j * hd:(j + 1) * hd]] * 2, axis=1) for j in range(2)]
            pairs = []
            for p in range(N_HEAD_C // 2):
                kvh = (2 * p) // (N_HEAD_C // 2)
                top = rows2 < BLK_C
                slope = jnp.where(top, 2.0 ** (-(2 * p + 1)), 2.0 ** (-(2 * p + 2)))
                sink = jnp.where(top, sink_ref[0, 2 * p], sink_ref[0, 2 * p + 1])

                def bias(sc, slope=slope, dist2=dist2, valid2=valid2):
                    return jnp.where(valid2, sc - slope * dist2, -jnp.inf)

                pairs.append(_pair_attention(q_c[rs, p * LANES:(p + 1) * LANES], kdup[kvh], vdup[kvh],
                                             hd ** -0.5, bias, sink))
                yield
            yc_blocks.append(jnp.concatenate(pairs, axis=1))
        last = slice(tq - BLK_C, tq)
        swa_k[...] = k_c[last]
        swa_v[...] = v_c[last]
        swa_p[...] = jnp.broadcast_to(pos_r[:, last], swa_p.shape)
        yc = jnp.concatenate(yc_blocks, axis=0) if len(yc_blocks) > 1 else yc_blocks[0]
        branch_out["c"] = yc * _silu(proj("c_z", W_C))

    def mixer_d():
        lbp = hlb_ref[...]
        e_lb = jnp.exp(lbp - jnp.max(lbp, axis=0, keepdims=True))
        p_lb = e_lb / jnp.sum(e_lb, axis=0, keepdims=True)
        lb = jnp.zeros((1, W_D), F32)
        for j in range(1, layer + 1):
            lb = lb + p_lb[j:j + 1, :]
        ls_f = jnp.log1p(-lb) + _log_sigmoid(proj("d_f", W_D))
        log_lb = jnp.log(lb)
        mx = jnp.maximum(log_lb, ls_f)
        log_f = mx + jnp.log(1.0 + jnp.exp(-jnp.abs(log_lb - ls_f)))
        k_d = 1.0 - jnp.exp(log_f)
        yield
        ri_d = lax.broadcasted_iota(jnp.int32, (tq, tq), 0)
        ci_d = lax.broadcasted_iota(jnp.int32, (tq, tq), 1)
        tri_d = jnp.where(ci_d <= ri_d, 1.0, 0.0).astype(BF16)
        b_all = _cumsum_rows(tri_d, log_f)
        qs_d = _silu(proj("d_q", W_D)) * (LANES ** -0.5)
        v_d = proj("d_i", W_D)
        yield
        hd_heads = []
        for h in range(N_HEAD_D):
            ls = slice(h * LANES, (h + 1) * LANES)
            o_h, st_new = yield from _hgrn_head(qs_d[:, ls], k_d[:, ls], v_d[:, ls], b_all[:, ls], h_st[h],
                                                ri_d, ci_d)
            h_st[h] = st_new
            hd_heads.append(o_h * lax.rsqrt(jnp.mean(o_h * o_h, axis=-1, keepdims=True) + EPS))
            yield
        branch_out["d"] = jnp.concatenate(hd_heads, axis=1) * hng_ref[...] * _silu(proj("d_z", W_D))

    def mixer_e():
        q_e = proj("e_q", W_E)
        pairs = []
        for p in range(W_E // LANES):
            ls = slice(p * LANES, (p + 1) * LANES)
            pairs.append(_pair_attention(q_e[:, ls], kv_ref[:, ls], kv_ref[:, W_E + p * LANES:W_E + (p + 1) * LANES],
                                         hd ** -0.5, lambda sc: sc, None))
            yield
        branch_out["e"] = jnp.concatenate(pairs, axis=1) * _silu(proj("e_z", W_E))

    yields_per_head_d = (tq // 2 // DIAG_D).bit_length() + 2
    heads_d = [0, 0] + ([1] + [0] * (yields_per_head_d - 1)) * N_HEAD_D
    _run_with_fill([[(mixer_a(), [0] + [1] * (N_HEAD_A * (tq // L_A)))],
                    [(mixer_b(), [1, 1, 1, 1])],
                    [(mixer_c(), [1] * (N_HEAD_C // 2 * (tq // BLK_C)))],
                    [(mixer_d(), heads_d)],
                    [(mixer_e(), [])]], issue_pending)

    issue_pending(len(pending))
    y_pieces = []
    for hf in range(n_gate_pieces):
        cs = slice(hf * gate_piece, (hf + 1) * gate_piece)
        y = None
        row0 = 0
        for j, (name, wj) in enumerate(zip("abcde", BRANCH_W)):
            t_j = issued[("gate", j, hf)] * _dot(_bf(branch_out[name]), wbr_ref[row0:row0 + wj, cs])
            y = t_j if y is None else y + t_j
            row0 += wj
        y_pieces.append(_bf(y))
    x_new = x + _dot(jnp.concatenate(y_pieces, axis=1), wout_ref[...])
    if final:
        x_new = _rmsnorm(x_new, fg_ref[...])
    o_ref[0] = x_new


def _const_spec(shape):
    nd = len(shape)
    return pl.BlockSpec(shape, lambda b, c: (0,) * nd, pipeline_mode=pl.Buffered(1))


_STACKED = ("w_in", "w_lru", "w_br", "w_out")


def _layer_spec(shape, layer):
    nd = len(shape)
    return pl.BlockSpec((None,) + tuple(shape[1:]), lambda b, c: (layer,) + (0,) * (nd - 1),
                        pipeline_mode=pl.Buffered(1))


def _layer_call(x, pos_col, pos_row, mem_kv, params, *, layer, final, tq):
    bsz, seq, d_model = x.shape
    in_specs = [
        pl.BlockSpec((1, tq, d_model), lambda b, c: (b, c, 0)),
        pl.BlockSpec((1, tq, 1), lambda b, c: (b, c, 0)),
        pl.BlockSpec((1, 1, tq), lambda b, c: (b, 0, c)),
        pl.BlockSpec((None, None) + tuple(mem_kv.shape[2:]), lambda b, c: (layer, b, 0, 0)),
    ]
    for name, p in params:
        if name == "sinks":
            in_specs.append(pl.BlockSpec(memory_space=pltpu.SMEM))
        elif name in _STACKED:
            in_specs.append(_layer_spec(p.shape, layer))
        else:
            in_specs.append(_const_spec(p.shape))
    scratch = [
        pltpu.VMEM((tq + SUBLANES, 2 * W_A), F32),
        pltpu.VMEM((tq + SUBLANES, W_B), F32),
        pltpu.VMEM((N_HEAD_A, LANES, LANES), F32),
        pltpu.VMEM((N_HEAD_A, SUBLANES, LANES), F32),
        pltpu.VMEM((N_HEAD_A, SUBLANES, LANES), F32),
        pltpu.VMEM((SUBLANES, W_B), F32),
        pltpu.VMEM((BLK_C, KV_W_C), BF16),
        pltpu.VMEM((BLK_C, KV_W_C), BF16),
        pltpu.VMEM((SUBLANES, BLK_C), jnp.int32),
        pltpu.VMEM((N_HEAD_D, LANES, LANES), F32),
    ]
    return pl.pallas_call(
        functools.partial(_layer_kernel, layer=layer, final=final, tq=tq),
        grid=(bsz, seq // tq),
        in_specs=in_specs,
        out_specs=pl.BlockSpec((1, tq, d_model), lambda b, c: (b, c, 0)),
        out_shape=jax.ShapeDtypeStruct(x.shape, x.dtype),
        scratch_shapes=scratch,
        compiler_params=pltpu.CompilerParams(dimension_semantics=("arbitrary", "arbitrary"),
                                             vmem_limit_bytes=VMEM_LIMIT_BYTES),
        name=f"hybrid_layer_{layer}",
    )(x, pos_col, pos_row, mem_kv, *[p for _, p in params])


def _tile_rows(seq):
    for t in (256, 128):
        if seq % t == 0:
            return t
    raise ValueError("sequence length must be a multiple of 128")


def kernel(x, mem, positions, norm_g, w_in, mlstm_conv_w, mlstm_b_if, mlstm_norm_g, lru_conv_w, lru_conv_b, lru_wa, lru_ba, lru_wx, lru_bx, lru_lambda, swa_sinks, hgrn_lb, hgrn_norm_g, mem_norm_g, w_mem_kv, w_br, w_out, final_norm_g):
    depth, d_model, n_in = w_in.shape
    n_a = 5 * W_A
    assert n_in == n_a + N_IF + 2 * W_B + 2 * W_C + 2 * KV_W_C + 4 * W_D + 2 * W_E + N_BRANCH * d_model
    tq = _tile_rows(x.shape[1])

    w_in_p = _w_in_prep_call(w_in, n_a)
    b_if = jnp.pad(mlstm_b_if, ((0, 0), (0, LANES - N_IF)))[:, None, :]
    nb, bd = lru_wa.shape[1], lru_wa.shape[2]
    eye = jnp.eye(nb, dtype=lru_wa.dtype)

    def block_diag(w):
        return (eye[None, :, None, :, None] * w[:, :, :, None, :]).reshape(depth, nb * bd, nb * bd)

    wa_d, wx_d = block_diag(lru_wa), block_diag(lru_wx)
    hw = W_B // 2
    w_lru = jnp.stack([jnp.concatenate([wa_d[:, :hw, :hw], wx_d[:, :hw, :hw]], axis=-1),
                       jnp.concatenate([wa_d[:, hw:, hw:], wx_d[:, hw:, hw:]], axis=-1)], axis=1).astype(BF16)
    mem_kv = _mem_kv_call(mem, mem_norm_g, w_mem_kv.astype(BF16))
    w_br_b = w_br.astype(BF16)
    w_out_b = w_out.astype(BF16)

    pos_col = positions[:, :, None]
    pos_row = positions[:, None, :]
    row = lambda a, l: a[l][None, :]
    for l in range(depth):
        params = [
            ("norm_g", row(norm_g, l)), ("w_in", w_in_p), ("conv_a", mlstm_conv_w[l]), ("b_if", b_if[l]),
            ("mlstm_norm_g", row(mlstm_norm_g, l)), ("lru_conv_w", lru_conv_w[l]), ("lru_conv_b", row(lru_conv_b, l)),
            ("w_lru", w_lru), ("lru_ba", row(lru_ba, l)), ("lru_bx", row(lru_bx, l)), ("lru_lambda", row(lru_lambda, l)),
            ("sinks", row(swa_sinks, l)), ("hgrn_lb", hgrn_lb), ("hgrn_norm_g", row(hgrn_norm_g, l)),
            ("w_br", w_br_b), ("w_out", w_out_b), ("final_norm_g", final_norm_g[None, :]),
        ]
        x = _layer_call(x, pos_col, pos_row, mem_kv, params, layer=l, final=(l == depth - 1), tq=tq)
    return x
```

```python
import functools
import math

import jax
import jax.numpy as jnp
from jax import lax
from jax.experimental import pallas as pl
from jax.experimental.pallas import tpu as pltpu

F32 = jnp.float32
BF16 = jnp.bfloat16

EPS = 1e-6
CONV_W = 4
N_HEAD_A = 4
W_A = 512
L_A = 128
INIT_M = -1e30
W_B = 512
LRU_C = 8.0
N_HEAD_C = 8
W_C = 512
KV_W_C = 128
BLK_C = 128
N_HEAD_D = 4
W_D = 512
DIAG_D = 8
W_E = 256
N_BRANCH = 5
BRANCH_W = (W_A, W_B, W_C, W_D, W_E)

LANES = 128
SUBLANES = 8
VMEM_LIMIT_BYTES = 58 * 1024 * 1024

_OFF = {}
_acc = 0
for _name, _w in (("a_qk", 2 * W_A), ("a_v", W_A), ("a_o", W_A), ("a_z", W_A), ("a_if", LANES),
                  ("b_x", W_B), ("b_z", W_B),
                  ("c_q", W_C), ("c_k", KV_W_C), ("c_v", KV_W_C), ("c_z", W_C),
                  ("d_q", W_D), ("d_f", W_D), ("d_i", W_D), ("d_z", W_D),
                  ("e_q", W_E), ("e_z", W_E), ("gates", None)):
    _OFF[_name] = _acc
    if _w is not None:
        _acc += _w
N_IF = 2 * N_HEAD_A


def _bf(x):
    return x.astype(BF16)


def _dot(a, b):
    return jnp.dot(a, b, preferred_element_type=F32)


def _dot_nt(a, b):
    return lax.dot_general(a, b, (((1,), (1,)), ((), ())), preferred_element_type=F32)


def _dot_tn(a, b):
    return lax.dot_general(a, b, (((0,), (0,)), ((), ())), preferred_element_type=F32)


def _silu(x):
    return x * jax.nn.sigmoid(x)


def _log_sigmoid(x):
    return jnp.minimum(x, 0.0) - jnp.log(1.0 + jnp.exp(-jnp.abs(x)))


def _softplus(x):
    return jnp.maximum(x, 0.0) + jnp.log(1.0 + jnp.exp(-jnp.abs(x)))


def _linear_scan_rows(a, b, h0):
    t, w = a.shape
    groups = t // SUBLANES
    a3 = a.reshape(groups, SUBLANES, w)
    b3 = b.reshape(groups, SUBLANES, w)
    sub = lax.broadcasted_iota(jnp.int32, a3.shape, 1)
    s = 1
    while s < SUBLANES:
        keep = sub >= s
        a_s = jnp.where(keep, pltpu.roll(a3, s, 1), 1.0)
        b_s = jnp.where(keep, pltpu.roll(b3, s, 1), 0.0)
        b3 = a3 * b_s + b3
        a3 = a3 * a_s
        s *= 2
    yield
    carry = h0
    out = []
    for g in range(groups):
        hg = b3[g] + a3[g] * carry
        out.append(hg)
        carry = hg[SUBLANES - 1:SUBLANES, :]
        if g % SUBLANES == SUBLANES - 1:
            yield
    return jnp.concatenate(out, axis=0)


def _rmsnorm(x, g):
    return x * lax.rsqrt(jnp.mean(x * x, axis=-1, keepdims=True) + EPS) * g


def _cumsum_rows(tri, x):
    hi = _bf(x)
    r1 = x - hi.astype(F32)
    mid = _bf(r1)
    lo = _bf(r1 - mid.astype(F32))
    return _dot(tri, hi) + _dot(tri, mid) + _dot(tri, lo)


def _block_row_bcast(b, blk, idx):
    n = b.shape[0] // blk
    b3 = b.reshape(n, blk, b.shape[1])
    return jnp.broadcast_to(b3[:, idx:idx + 1, :], b3.shape).reshape(b.shape)


def _causal_conv(tail, x, w_ref, t):
    groups = t // SUBLANES
    x3 = x.reshape(groups, SUBLANES, x.shape[1])
    sub = lax.broadcasted_iota(jnp.int32, x3.shape, 1)
    tail3 = tail[...][None]
    y = w_ref[CONV_W - 1:CONV_W, :] * x3
    for s in range(1, CONV_W):
        rolled = pltpu.roll(x3, s, 1)
        rolled_prev = jnp.concatenate([pltpu.roll(tail3, s, 1), rolled[:-1]], axis=0)
        y = y + w_ref[CONV_W - 1 - s:CONV_W - s, :] * jnp.where(sub >= s, rolled, rolled_prev)
    tail[...] = x3[groups - 1]
    return y.reshape(x.shape)


def _mlstm_head(q, k, v, g, icol, grow, irow, c_st, n_st, m_st, causal):
    L = q.shape[0]
    dm = jnp.where(causal, g - grow + irow, -jnp.inf)
    g_end = g[L - 1:L, :]
    w_end = g_end - g + icol
    m_loc = jnp.max(w_end, axis=0, keepdims=True)
    ke = k * jnp.exp(w_end - m_loc)
    vb = _bf(v)
    c_loc = _dot_tn(_bf(ke), vb)
    n_loc = jnp.sum(ke, axis=0, keepdims=True)

    a_inter = g + m_st
    m_j = jnp.maximum(a_inter, jnp.max(dm, axis=-1, keepdims=True))
    qb = _bf(q)
    p = jnp.exp(dm - m_j) * _dot_nt(qb, _bf(k))
    w_inter = jnp.exp(a_inter - m_j)
    num = _dot(_bf(p), vb) + w_inter * _dot(qb, _bf(c_st))
    den = jnp.sum(p, axis=-1, keepdims=True) + w_inter * jnp.sum(q * n_st, axis=-1, keepdims=True)
    h = num / jnp.maximum(jnp.abs(den), jnp.exp(-m_j))

    m_new = jnp.maximum(g_end + m_st, m_loc)
    a = jnp.exp(g_end + m_st - m_new)
    b = jnp.exp(m_loc - m_new)
    return h, a * c_st + b * c_loc, a * n_st + b * n_loc, m_new


def _hgrn_head(qs, kk, v, b, st_t, ri, ci):
    t = qs.shape[0]
    rows = ri[:, 0:1]
    o = _dot_nt(_bf(qs * jnp.exp(b)), _bf(st_t))
    b_end = b[t - 1:t, :]
    vb = _bf(v)
    st_new = st_t * jnp.exp(b_end) + _dot_tn(vb, _bf(kk * jnp.exp(b_end - b)))
    yield

    amat = None
    s = t // 2
    while s >= DIAG_D:
        sh = s.bit_length() - 1
        e = jnp.exp(-jnp.abs(b - _block_row_bcast(b, 2 * s, s - 1)))
        upper = (rows & (2 * s - 1)) >= s
        xl = _bf(jnp.where(upper, qs, kk) * e)
        al = _dot_nt(xl, xl)
        bj = ri >> sh
        wanted = (((bj - (ci >> sh)) << 2) + (bj & 1)) == 5
        amat = jnp.where(wanted, al, 0.0 if amat is None else amat)
        s //= 2
        yield
    sh = DIAG_D.bit_length() - 1
    d = b - _block_row_bcast(b, DIAG_D, 0)
    ad = _dot_nt(_bf(qs * jnp.exp(d)), _bf(kk * jnp.exp(-d)))
    diag = ((ri >> sh) == (ci >> sh)) & (ci <= ri)
    amat = jnp.where(diag, ad, amat)
    return o + _dot(_bf(amat), vb), st_new


def _pair_attention(qp, kmat, vmat, scale, bias_fn, sink_col):
    t = qp.shape[0]
    lane = lax.broadcasted_iota(jnp.int32, qp.shape, 1)
    lo = lane < (LANES // 2)
    assert math.log2(scale).is_integer()
    qs = qp * scale
    q2 = jnp.concatenate([jnp.where(lo, qs, 0.0), jnp.where(lo, 0.0, qs)], axis=0)
    s = bias_fn(_dot_nt(_bf(q2), kmat))
    m = jnp.max(s, axis=-1, keepdims=True)
    if sink_col is not None:
        m = jnp.maximum(m, sink_col)
    p = jnp.exp(s - m)
    den = jnp.sum(p, axis=-1, keepdims=True)
    if sink_col is not None:
        den = den + jnp.exp(sink_col - m)
    o2 = _dot(_bf(p), vmat) / den
    return jnp.where(lo, o2[:t], o2[t:])


def _run_with_fill(tasks, issue):
    for group in tasks:
        active = [(gen, pieces, [0]) for gen, pieces in group]
        while active:
            for task in list(active):
                gen, pieces, count = task
                try:
                    next(gen)
                except StopIteration:
                    active.remove(task)
                    continue
                if count[0] < len(pieces):
                    issue(pieces[count[0]])
                count[0] += 1


PREP_COLS = 384


def _w_in_prep_kernel(wt_ref, o_ref, *, gate_step, gate_rows):
    t = wt_ref[0]
    row = lax.broadcasted_iota(jnp.int32, t.shape, 0)
    keep = (pl.program_id(1) != gate_step) | (row < gate_rows)
    o_ref[...] = _bf(jnp.where(keep, t, 0.0).T)


def _w_in_prep_call(w_in, n_a):
    depth, d_model, n_in = w_in.shape
    n_pad = n_in + LANES - N_IF
    gate_end = n_a + LANES
    assert gate_end % PREP_COLS == 0 and n_pad % PREP_COLS == 0
    gate_step = gate_end // PREP_COLS - 1
    shift = (LANES - N_IF) // SUBLANES

    def window(l, g):
        return l, (g * (PREP_COLS // SUBLANES) - jnp.where(g <= gate_step, 0, shift)) * SUBLANES, 0

    return pl.pallas_call(
        functools.partial(_w_in_prep_kernel, gate_step=gate_step, gate_rows=(n_a + N_IF) % PREP_COLS),
        grid=(depth, n_pad // PREP_COLS),
        in_specs=[pl.BlockSpec((pl.Element(1), pl.Element(PREP_COLS), pl.Element(d_model)), window)],
        out_specs=pl.BlockSpec((None, d_model, PREP_COLS), lambda l, g: (l, 0, g)),
        out_shape=jax.ShapeDtypeStruct((depth, d_model, n_pad), BF16),
        compiler_params=pltpu.CompilerParams(dimension_semantics=("arbitrary", "arbitrary")),
        name="w_in_prep",
    )(jnp.swapaxes(w_in, 1, 2))


def _mem_kv_kernel(mem_ref, g_ref, w_ref, o_ref):
    o_ref[...] = _bf(_dot(_bf(_rmsnorm(mem_ref[0], g_ref[...])), w_ref[...]))


def _mem_kv_call(mem, mem_norm_g, w_kv):
    bsz, m_len, d_model = mem.shape
    depth, _, n_kv = w_kv.shape
    return pl.pallas_call(
        _mem_kv_kernel,
        grid=(depth, bsz),
        in_specs=[pl.BlockSpec((1, m_len, d_model), lambda l, b: (b, 0, 0)),
                  pl.BlockSpec((None, 1, d_model), lambda l, b: (l, 0, 0)),
                  pl.BlockSpec((None, d_model, n_kv), lambda l, b: (l, 0, 0))],
        out_specs=pl.BlockSpec((None, None, m_len, n_kv), lambda l, b: (l, b, 0, 0)),
        out_shape=jax.ShapeDtypeStruct((depth, bsz, m_len, n_kv), BF16),
        compiler_params=pltpu.CompilerParams(dimension_semantics=("arbitrary", "arbitrary")),
        name="mem_kv",
    )(mem, mem_norm_g[:, None, :], w_kv)


def _layer_kernel(x_ref, pc_ref, pr_ref, kv_ref, ng_ref, win_ref, cwa_ref, bif_ref, mng_ref,
                  lcw_ref, lcb_ref, wlru_ref, lba_ref, lbx_ref, lam_ref, sink_ref, hlb_ref, hng_ref,
                  wbr_ref, wout_ref, fg_ref,
                  o_ref,
                  conv_a, conv_b, m_c, m_n, m_m, lru_h, swa_k, swa_v, swa_p, h_st,
                  *, layer, final, tq):
    c = pl.program_id(1)
    d_model = x_ref.shape[-1]

    @pl.when(c == 0)
    def _init():
        conv_a[...] = jnp.zeros(conv_a.shape, F32)
        conv_b[...] = jnp.zeros(conv_b.shape, F32)
        m_c[...] = jnp.zeros(m_c.shape, F32)
        m_n[...] = jnp.zeros(m_n.shape, F32)
        m_m[...] = jnp.full(m_m.shape, INIT_M, F32)
        lru_h[...] = jnp.zeros(lru_h.shape, F32)
        swa_k[...] = jnp.zeros(swa_k.shape, BF16)
        swa_v[...] = jnp.zeros(swa_v.shape, BF16)
        swa_p[...] = jnp.zeros(swa_p.shape, jnp.int32)
        h_st[...] = jnp.zeros(h_st.shape, F32)

    x = x_ref[0]
    hb = _bf(_rmsnorm(x, ng_ref[...]))

    gate_piece = d_model // 2
    n_gate_pieces = d_model // gate_piece
    widths = {"a_o": W_A, "a_z": W_A, "b_x": W_B, "b_z": W_B, "c_q": W_C, "c_kv": 2 * KV_W_C, "c_z": W_C,
              "d_q": W_D, "d_f": W_D, "d_i": W_D, "d_z": W_D, "e_q": W_E, "e_z": W_E}
    offsets = dict(_OFF, c_kv=_OFF["c_k"])
    gates = [("gate", j, hf) for j in range(N_BRANCH) for hf in range(n_gate_pieces)]
    pending = (["a_o", "a_z", "b_x", "b_z", "c_q", "c_kv"] + gates[:2] + ["c_z", "d_f", "d_q", "d_i"]
               + ["d_z", "e_q", "e_z"] + gates[2:])
    issued = {}

    def issue(piece):
        if isinstance(piece, tuple):
            _, j, hf = piece
            g0 = _OFF["gates"] + j * d_model + hf * gate_piece
            issued[piece] = jax.nn.sigmoid(_dot(hb, win_ref[:, g0:g0 + gate_piece]))
        else:
            o = offsets[piece]
            issued[piece] = _dot(hb, win_ref[:, o:o + widths[piece]])

    def issue_pending(n):
        for _ in range(min(n, len(pending))):
            issue(pending.pop(0))

    def proj(name, width):
        if name in widths:
            assert widths[name] == width
            if name not in issued:
                pending.remove(name)
                issue(name)
            return issued[name]
        o = _OFF[name]
        return _dot(hb, win_ref[:, o:o + width])

    branch_out = {}
    hd = LANES // 2

    def mixer_a():
        qk = _silu(_causal_conv(conv_a, proj("a_qk", 2 * W_A), cwa_ref, tq))
        yield
        v_a = proj("a_v", W_A)
        gif = proj("a_if", LANES) + bif_ref[...]
        logf = _log_sigmoid(gif)
        ri_a = lax.broadcasted_iota(jnp.int32, (L_A, L_A), 0)
        ci_a = lax.broadcasted_iota(jnp.int32, (L_A, L_A), 1)
        causal_a = ci_a <= ri_a
        tri_a = jnp.where(causal_a, 1.0, 0.0).astype(BF16)
        hm_chunks = []
        for n in range(tq // L_A):
            rs = slice(n * L_A, (n + 1) * L_A)
            gif_c = gif[rs]
            g_all = _cumsum_rows(tri_a, logf[rs])
            g_t = g_all.T
            gif_t = gif_c.T
            heads = []
            for h in range(N_HEAD_A):
                ls = slice(h * LANES, (h + 1) * LANES)
                hh, c_new, n_new, m_new = _mlstm_head(
                    qk[rs, ls] * (LANES ** -0.5), qk[rs, W_A + h * LANES:W_A + (h + 1) * LANES], v_a[rs, ls],
                    g_all[:, N_HEAD_A + h:N_HEAD_A + h + 1], gif_c[:, h:h + 1],
                    g_t[N_HEAD_A + h:N_HEAD_A + h + 1, :], gif_t[h:h + 1, :],
                    m_c[h], m_n[h, 0:1, :], m_m[h, 0:1, 0:1], causal_a)
                m_c[h] = c_new
                m_n[h] = jnp.broadcast_to(n_new, m_n.shape[1:])
                m_m[h] = jnp.broadcast_to(m_new, m_m.shape[1:])
                hh = hh * lax.rsqrt(jnp.mean(hh * hh, axis=-1, keepdims=True) + EPS)
                heads.append(hh)
                yield
            hm_chunks.append(jnp.concatenate(heads, axis=1))
        hm = jnp.concatenate(hm_chunks, axis=0) if len(hm_chunks) > 1 else hm_chunks[0]
        branch_out["a"] = hm * mng_ref[...] * jax.nn.sigmoid(proj("a_o", W_A)) * _silu(proj("a_z", W_A))

    def mixer_b():
        xc = _causal_conv(conv_b, proj("b_x", W_B), lcw_ref, tq) + lcb_ref[...]
        yield
        xcb = _bf(xc)
        half = W_B // 2
        rx0 = _dot(xcb[:, :half], wlru_ref[0])
        rx1 = _dot(xcb[:, half:], wlru_ref[1])
        r = jax.nn.sigmoid(jnp.concatenate([rx0[:, :half], rx1[:, :half]], axis=1) + lba_ref[...])
        i_g = jax.nn.sigmoid(jnp.concatenate([rx0[:, half:], rx1[:, half:]], axis=1) + lbx_ref[...])
        yield
        a_t = jnp.exp(-LRU_C * r * _softplus(-lam_ref[...]))
        b_t = jnp.sqrt(1.0 - a_t * a_t) * (i_g * xc)
        yield
        h_lru = yield from _linear_scan_rows(a_t, b_t, lru_h[0:1, :])
        lru_h[...] = jnp.broadcast_to(h_lru[tq - 1:tq, :], lru_h.shape)
        branch_out["b"] = h_lru * _silu(proj("b_z", W_B))

    def mixer_c():
        q_c = proj("c_q", W_C)
        kv_c = _bf(proj("c_kv", 2 * KV_W_C))
        k_c = kv_c[:, :KV_W_C]
        v_c = kv_c[:, KV_W_C:]
        pos_c = pc_ref[0]
        pos_r = pr_ref[0]
        rows2 = lax.broadcasted_iota(jnp.int32, (2 * BLK_C, 1), 0)
        ri_c = lax.broadcasted_iota(jnp.int32, (BLK_C, 2 * BLK_C), 0)
        ci_c = lax.broadcasted_iota(jnp.int32, (BLK_C, 2 * BLK_C), 1)
        rel = ri_c + BLK_C - ci_c
        in_window = (rel >= 0) & (rel < BLK_C)
        yc_blocks = []
        for n in range(tq // BLK_C):
            rs = slice(n * BLK_C, (n + 1) * BLK_C)
            k_prev = swa_k[...] if n == 0 else k_c[(n - 1) * BLK_C:n * BLK_C]
            v_prev = swa_v[...] if n == 0 else v_c[(n - 1) * BLK_C:n * BLK_C]
            p_prev = swa_p[0:1, :] if n == 0 else pos_r[:, (n - 1) * BLK_C:n * BLK_C]
            kk = jnp.concatenate([k_prev, k_c[rs]], axis=0)
            vv = jnp.concatenate([v_prev, v_c[rs]], axis=0)
            pk = jnp.concatenate([p_prev, pos_r[:, rs]], axis=1)
            dist = jnp.abs(pos_c[rs] - pk).astype(F32)
            first_key = jnp.where((c * (tq // BLK_C) + n) == 0, BLK_C, 0)
            valid = in_window & (ci_c >= first_key)
            dist2 = jnp.concatenate([dist, dist], axis=0)
            valid2 = jnp.concatenate([valid, valid], axis=0)
            kdup = [jnp.concatenate([kk[:, j * hd:(j + 1) * hd]] * 2, axis=1) for j in range(2)]
            vdup = [jnp.concatenate([vv[:, j * hd:(j + 1) * hd]] * 2, axis=1) for j in range(2)]
            pairs = []
            for p in range(N_HEAD_C // 2):
                kvh = (2 * p) // (N_HEAD_C // 2)
                top = rows2 < BLK_C
                slope = jnp.where(top, 2.0 ** (-(2 * p + 1)), 2.0 ** (-(2 * p + 2)))
                sink = jnp.where(top, sink_ref[0, 2 * p], sink_ref[0, 2 * p + 1])

                def bias(sc, slope=slope, dist2=dist2, valid2=valid2):
                    return jnp.where(valid2, sc - slope * dist2, -jnp.inf)

                pairs.append(_pair_attention(q_c[rs, p * LANES:(p + 1) * LANES], kdup[kvh], vdup[kvh],
                                             hd ** -0.5, bias, sink))
                yield
            yc_blocks.append(jnp.concatenate(pairs, axis=1))
        last = slice(tq - BLK_C, tq)
        swa_k[...] = k_c[last]
        swa_v[...] = v_c[last]
        swa_p[...] = jnp.broadcast_to(pos_r[:, last], swa_p.shape)
        yc = jnp.concatenate(yc_blocks, axis=0) if len(yc_blocks) > 1 else yc_blocks[0]
        branch_out["c"] = yc * _silu(proj("c_z", W_C))

    def mixer_d():
        lbp = hlb_ref[...]
        e_lb = jnp.exp(lbp - jnp.max(lbp, axis=0, keepdims=True))
        p_lb = e_lb / jnp.sum(e_lb, axis=0, keepdims=True)
        lb = jnp.zeros((1, W_D), F32)
        for j in range(1, layer + 1):
            lb = lb + p_lb[j:j + 1, :]
        ls_f = jnp.log1p(-lb) + _log_sigmoid(proj("d_f", W_D))
        log_lb = jnp.log(lb)
        mx = jnp.maximum(log_lb, ls_f)
        log_f = mx + jnp.log(1.0 + jnp.exp(-jnp.abs(log_lb - ls_f)))
        k_d = 1.0 - jnp.exp(log_f)
        yield
        ri_d = lax.broadcasted_iota(jnp.int32, (tq, tq), 0)
        ci_d = lax.broadcasted_iota(jnp.int32, (tq, tq), 1)
        tri_d = jnp.where(ci_d <= ri_d, 1.0, 0.0).astype(BF16)
        b_all = _cumsum_rows(tri_d, log_f)
        qs_d = _silu(proj("d_q", W_D)) * (LANES ** -0.5)
        v_d = proj("d_i", W_D)
        yield
        hd_heads = []
        for h in range(N_HEAD_D):
            ls = slice(h * LANES, (h + 1) * LANES)
            o_h, st_new = yield from _hgrn_head(qs_d[:, ls], k_d[:, ls], v_d[:, ls], b_all[:, ls], h_st[h],
                                                ri_d, ci_d)
            h_st[h] = st_new
            hd_heads.append(o_h * lax.rsqrt(jnp.mean(o_h * o_h, axis=-1, keepdims=True) + EPS))
            yield
        branch_out["d"] = jnp.concatenate(hd_heads, axis=1) * hng_ref[...] * _silu(proj("d_z", W_D))

    def mixer_e():
        q_e = proj("e_q", W_E)
        pairs = []
        for p in range(W_E // LANES):
            ls = slice(p * LANES, (p + 1) * LANES)
            pairs.append(_pair_attention(q_e[:, ls], kv_ref[:, ls], kv_ref[:, W_E + p * LANES:W_E + (p + 1) * LANES],
                                         hd ** -0.5, lambda sc: sc, None))
            yield
        branch_out["e"] = jnp.concatenate(pairs, axis=1) * _silu(proj("e_z", W_E))

    yields_per_head_d = (tq // 2 // DIAG_D).bit_length() + 2
    heads_d = [0, 0] + ([1] + [0] * (yields_per_head_d - 1)) * N_HEAD_D
    _run_with_fill([[(mixer_a(), [0] + [1] * (N_HEAD_A * (tq // L_A)))],
                    [(mixer_b(), [1, 1, 1, 1])],
                    [(mixer_c(), [1] * (N_HEAD_C // 2 * (tq // BLK_C)))],
                    [(mixer_d(), heads_d)],
                    [(mixer_e(), [])]], issue_pending)

    issue_pending(len(pending))
    y_pieces = []
    for hf in range(n_gate_pieces):
        cs = slice(hf * gate_piece, (hf + 1) * gate_piece)
        y = None
        row0 = 0
        for j, (name, wj) in enumerate(zip("abcde", BRANCH_W)):
            t_j = issued[("gate", j, hf)] * _dot(_bf(branch_out[name]), wbr_ref[row0:row0 + wj, cs])
            y = t_j if y is None else y + t_j
            row0 += wj
        y_pieces.append(_bf(y))
    x_new = x + _dot(jnp.concatenate(y_pieces, axis=1), wout_ref[...])
    if final:
        x_new = _rmsnorm(x_new, fg_ref[...])
    o_ref[0] = x_new


def _const_spec(shape):
    nd = len(shape)
    return pl.BlockSpec(shape, lambda b, c: (0,) * nd, pipeline_mode=pl.Buffered(1))


_STACKED = ("w_in", "w_lru", "w_br", "w_out")


def _layer_spec(shape, layer):
    nd = len(shape)
    return pl.BlockSpec((None,) + tuple(shape[1:]), lambda b, c: (layer,) + (0,) * (nd - 1),
                        pipeline_mode=pl.Buffered(1))


def _layer_call(x, pos_col, pos_row, mem_kv, params, *, layer, final, tq):
    bsz, seq, d_model = x.shape
    in_specs = [
        pl.BlockSpec((1, tq, d_model), lambda b, c: (b, c, 0)),
        pl.BlockSpec((1, tq, 1), lambda b, c: (b, c, 0)),
        pl.BlockSpec((1, 1, tq), lambda b, c: (b, 0, c)),
        pl.BlockSpec((None, None) + tuple(mem_kv.shape[2:]), lambda b, c: (layer, b, 0, 0)),
    ]
    for name, p in params:
        if name == "sinks":
            in_specs.append(pl.BlockSpec(memory_space=pltpu.SMEM))
        elif name in _STACKED:
            in_specs.append(_layer_spec(p.shape, layer))
        else:
            in_specs.append(_const_spec(p.shape))
    scratch = [
        pltpu.VMEM((SUBLANES, 2 * W_A), F32),
        pltpu.VMEM((SUBLANES, W_B), F32),
        pltpu.VMEM((N_HEAD_A, LANES, LANES), F32),
        pltpu.VMEM((N_HEAD_A, SUBLANES, LANES), F32),
        pltpu.VMEM((N_HEAD_A, SUBLANES, LANES), F32),
        pltpu.VMEM((SUBLANES, W_B), F32),
        pltpu.VMEM((BLK_C, KV_W_C), BF16),
        pltpu.VMEM((BLK_C, KV_W_C), BF16),
        pltpu.VMEM((SUBLANES, BLK_C), jnp.int32),
        pltpu.VMEM((N_HEAD_D, LANES, LANES), F32),
    ]
    return pl.pallas_call(
        functools.partial(_layer_kernel, layer=layer, final=final, tq=tq),
        grid=(bsz, seq // tq),
        in_specs=in_specs,
        out_specs=pl.BlockSpec((1, tq, d_model), lambda b, c: (b, c, 0)),
        out_shape=jax.ShapeDtypeStruct(x.shape, x.dtype),
        scratch_shapes=scratch,
        compiler_params=pltpu.CompilerParams(dimension_semantics=("arbitrary", "arbitrary"),
                                             vmem_limit_bytes=VMEM_LIMIT_BYTES),
        name=f"hybrid_layer_{layer}",
    )(x, pos_col, pos_row, mem_kv, *[p for _, p in params])


def _tile_rows(seq):
    for t in (256, 128):
        if seq % t == 0:
            return t
    raise ValueError("sequence length must be a multiple of 128")


def kernel(x, mem, positions, norm_g, w_in, mlstm_conv_w, mlstm_b_if, mlstm_norm_g, lru_conv_w, lru_conv_b, lru_wa, lru_ba, lru_wx, lru_bx, lru_lambda, swa_sinks, hgrn_lb, hgrn_norm_g, mem_norm_g, w_mem_kv, w_br, w_out, final_norm_g):
    depth, d_model, n_in = w_in.shape
    n_a = 5 * W_A
    assert n_in == n_a + N_IF + 2 * W_B + 2 * W_C + 2 * KV_W_C + 4 * W_D + 2 * W_E + N_BRANCH * d_model
    tq = _tile_rows(x.shape[1])

    w_in_p = _w_in_prep_call(w_in, n_a)
    b_if = jnp.pad(mlstm_b_if, ((0, 0), (0, LANES - N_IF)))[:, None, :]
    nb, bd = lru_wa.shape[1], lru_wa.shape[2]
    eye = jnp.eye(nb, dtype=lru_wa.dtype)

    def block_diag(w):
        return (eye[None, :, None, :, None] * w[:, :, :, None, :]).reshape(depth, nb * bd, nb * bd)

    wa_d, wx_d = block_diag(lru_wa), block_diag(lru_wx)
    hw = W_B // 2
    w_lru = jnp.stack([jnp.concatenate([wa_d[:, :hw, :hw], wx_d[:, :hw, :hw]], axis=-1),
                       jnp.concatenate([wa_d[:, hw:, hw:], wx_d[:, hw:, hw:]], axis=-1)], axis=1).astype(BF16)
    mem_kv = _mem_kv_call(mem, mem_norm_g, w_mem_kv.astype(BF16))
    w_br_b = w_br.astype(BF16)
    w_out_b = w_out.astype(BF16)

    pos_col = positions[:, :, None]
    pos_row = positions[:, None, :]
    row = lambda a, l: a[l][None, :]
    for l in range(depth):
        params = [
            ("norm_g", row(norm_g, l)), ("w_in", w_in_p), ("conv_a", mlstm_conv_w[l]), ("b_if", b_if[l]),
            ("mlstm_norm_g", row(mlstm_norm_g, l)), ("lru_conv_w", lru_conv_w[l]), ("lru_conv_b", row(lru_conv_b, l)),
            ("w_lru", w_lru), ("lru_ba", row(lru_ba, l)), ("lru_bx", row(lru_bx, l)), ("lru_lambda", row(lru_lambda, l)),
            ("sinks", row(swa_sinks, l)), ("hgrn_lb", hgrn_lb), ("hgrn_norm_g", row(hgrn_norm_g, l)),
            ("w_br", w_br_b), ("w_out", w_out_b), ("final_norm_g", final_norm_g[None, :]),
        ]
        x = _layer_call(x, pos_col, pos_row, mem_kv, params, layer=l, final=(l == depth - 1), tq=tq)
    return x
```

```python
import functools
import math

import jax
import jax.numpy as jnp
from jax import lax
from jax.experimental import pallas as pl
from jax.experimental.pallas import tpu as pltpu

F32 = jnp.float32
BF16 = jnp.bfloat16

EPS = 1e-6
CONV_W = 4
N_HEAD_A = 4
W_A = 512
L_A = 128
INIT_M = -1e30
W_B = 512
LRU_C = 8.0
N_HEAD_C = 8
W_C = 512
KV_W_C = 128
BLK_C = 128
N_HEAD_D = 4
W_D = 512
DIAG_D = 8
W_E = 256
N_BRANCH = 5
BRANCH_W = (W_A, W_B, W_C, W_D, W_E)

LANES = 128
SUBLANES = 8
VMEM_LIMIT_BYTES = 58 * 1024 * 1024

D_MODEL = 1024
N_IF = 2 * N_HEAD_A

_OFF = {}
_acc = 0
for _name, _w in (("b_x", W_B), ("b_z", W_B),
                  ("c_q", W_C), ("c_k", KV_W_C), ("c_v", KV_W_C), ("c_z", W_C),
                  ("d_q", W_D), ("d_f", W_D), ("d_i", W_D), ("d_z", W_D),
                  ("e_q", W_E), ("e_z", W_E), ("gates", N_BRANCH * D_MODEL),
                  ("a_qk", 2 * W_A), ("a_v", W_A), ("a_o", W_A), ("a_z", W_A), ("a_if", LANES)):
    _OFF[_name] = _acc
    _acc += _w
N_REST = _OFF["a_qk"]
W_IN_A = _acc - N_REST


def _bf(x):
    return x.astype(BF16)


def _dot(a, b):
    return jnp.dot(a, b, preferred_element_type=F32)


def _dot_nt(a, b):
    return lax.dot_general(a, b, (((1,), (1,)), ((), ())), preferred_element_type=F32)


def _dot_tn(a, b):
    return lax.dot_general(a, b, (((0,), (0,)), ((), ())), preferred_element_type=F32)


def _silu(x):
    return x * jax.nn.sigmoid(x)


def _log_sigmoid(x):
    return jnp.minimum(x, 0.0) - jnp.log(1.0 + jnp.exp(-jnp.abs(x)))


def _softplus(x):
    return jnp.maximum(x, 0.0) + jnp.log(1.0 + jnp.exp(-jnp.abs(x)))


def _linear_scan_rows(a, b, h0):
    t, w = a.shape
    groups = t // SUBLANES
    a3 = a.reshape(groups, SUBLANES, w)
    b3 = b.reshape(groups, SUBLANES, w)
    sub = lax.broadcasted_iota(jnp.int32, a3.shape, 1)
    s = 1
    while s < SUBLANES:
        keep = sub >= s
        a_s = jnp.where(keep, pltpu.roll(a3, s, 1), 1.0)
        b_s = jnp.where(keep, pltpu.roll(b3, s, 1), 0.0)
        b3 = a3 * b_s + b3
        a3 = a3 * a_s
        s *= 2
    yield
    carry = h0
    out = []
    for g in range(groups):
        hg = b3[g] + a3[g] * carry
        out.append(hg)
        carry = hg[SUBLANES - 1:SUBLANES, :]
        if g % SUBLANES == SUBLANES - 1:
            yield
    return jnp.concatenate(out, axis=0)


def _rmsnorm(x, g):
    return x * lax.rsqrt(jnp.mean(x * x, axis=-1, keepdims=True) + EPS) * g


def _cumsum_rows(tri, x):
    hi = _bf(x)
    r1 = x - hi.astype(F32)
    mid = _bf(r1)
    lo = _bf(r1 - mid.astype(F32))
    return _dot(tri, hi) + _dot(tri, mid) + _dot(tri, lo)


def _block_row_bcast(b, blk, idx):
    n = b.shape[0] // blk
    b3 = b.reshape(n, blk, b.shape[1])
    return jnp.broadcast_to(b3[:, idx:idx + 1, :], b3.shape).reshape(b.shape)


def _causal_conv(tail, x, w_ref, t):
    groups = t // SUBLANES
    x3 = x.reshape(groups, SUBLANES, x.shape[1])
    sub = lax.broadcasted_iota(jnp.int32, x3.shape, 1)
    tail3 = tail[...][None]
    y = w_ref[CONV_W - 1:CONV_W, :] * x3
    for s in range(1, CONV_W):
        rolled = pltpu.roll(x3, s, 1)
        rolled_prev = jnp.concatenate([pltpu.roll(tail3, s, 1), rolled[:-1]], axis=0)
        y = y + w_ref[CONV_W - 1 - s:CONV_W - s, :] * jnp.where(sub >= s, rolled, rolled_prev)
    tail[...] = x3[groups - 1]
    return y.reshape(x.shape)


def _mlstm_head(q, k, v, g, icol, grow, irow, c_st, n_st, m_st, causal):
    L = q.shape[0]
    dm = jnp.where(causal, g - grow + irow, -jnp.inf)
    g_end = g[L - 1:L, :]
    w_end = g_end - g + icol
    m_loc = jnp.max(w_end, axis=0, keepdims=True)
    ke = k * jnp.exp(w_end - m_loc)
    vb = _bf(v)
    c_loc = _dot_tn(_bf(ke), vb)
    n_loc = jnp.sum(ke, axis=0, keepdims=True)

    a_inter = g + m_st
    m_j = jnp.maximum(a_inter, jnp.max(dm, axis=-1, keepdims=True))
    qb = _bf(q)
    p = jnp.exp(dm - m_j) * _dot_nt(qb, _bf(k))
    w_inter = jnp.exp(a_inter - m_j)
    num = _dot(_bf(p), vb) + w_inter * _dot(qb, _bf(c_st))
    den = jnp.sum(p, axis=-1, keepdims=True) + w_inter * jnp.sum(q * n_st, axis=-1, keepdims=True)
    h = num / jnp.maximum(jnp.abs(den), jnp.exp(-m_j))

    m_new = jnp.maximum(g_end + m_st, m_loc)
    a = jnp.exp(g_end + m_st - m_new)
    b = jnp.exp(m_loc - m_new)
    return h, a * c_st + b * c_loc, a * n_st + b * n_loc, m_new


def _hgrn_head(qs, kk, v, b, st_t, ri, ci):
    t = qs.shape[0]
    rows = ri[:, 0:1]
    o = _dot_nt(_bf(qs * jnp.exp(b)), _bf(st_t))
    b_end = b[t - 1:t, :]
    vb = _bf(v)
    st_new = st_t * jnp.exp(b_end) + _dot_tn(vb, _bf(kk * jnp.exp(b_end - b)))
    yield

    amat = None
    s = t // 2
    while s >= DIAG_D:
        sh = s.bit_length() - 1
        e = jnp.exp(-jnp.abs(b - _block_row_bcast(b, 2 * s, s - 1)))
        upper = (rows & (2 * s - 1)) >= s
        xl = _bf(jnp.where(upper, qs, kk) * e)
        al = _dot_nt(xl, xl)
        bj = ri >> sh
        wanted = (((bj - (ci >> sh)) << 2) + (bj & 1)) == 5
        amat = jnp.where(wanted, al, 0.0 if amat is None else amat)
        s //= 2
        yield
    sh = DIAG_D.bit_length() - 1
    d = b - _block_row_bcast(b, DIAG_D, 0)
    ad = _dot_nt(_bf(qs * jnp.exp(d)), _bf(kk * jnp.exp(-d)))
    diag = ((ri >> sh) == (ci >> sh)) & (ci <= ri)
    amat = jnp.where(diag, ad, amat)
    return o + _dot(_bf(amat), vb), st_new


def _pair_attention(qp, kmat, vmat, scale, bias_fn, sink_col):
    t = qp.shape[0]
    lane = lax.broadcasted_iota(jnp.int32, qp.shape, 1)
    lo = lane < (LANES // 2)
    assert math.log2(scale).is_integer()
    qs = qp * scale
    q2 = jnp.concatenate([jnp.where(lo, qs, 0.0), jnp.where(lo, 0.0, qs)], axis=0)
    s = bias_fn(_dot_nt(_bf(q2), kmat))
    m = jnp.max(s, axis=-1, keepdims=True)
    if sink_col is not None:
        m = jnp.maximum(m, sink_col)
    p = jnp.exp(s - m)
    den = jnp.sum(p, axis=-1, keepdims=True)
    if sink_col is not None:
        den = den + jnp.exp(sink_col - m)
    o2 = _dot(_bf(p), vmat) / den
    return jnp.where(lo, o2[:t], o2[t:])


def _run_with_fill(tasks, issue):
    for group in tasks:
        active = [(gen, pieces, [0]) for gen, pieces in group]
        while active:
            for task in list(active):
                gen, pieces, count = task
                try:
                    next(gen)
                except StopIteration:
                    active.remove(task)
                    continue
                if count[0] < len(pieces):
                    issue(pieces[count[0]])
                count[0] += 1


PREP_COLS_A = 384
PREP_COLS_REST = 1664


def _w_in_prep_kernel(wt_ref, *refs, mask_step, keep_rows):
    t = wt_ref[0]
    if mask_step is not None:
        row = lax.broadcasted_iota(jnp.int32, t.shape, 0)
        t = jnp.where((pl.program_id(1) != mask_step) | (row < keep_rows), t, 0.0)
    refs[-1][...] = _bf(t.T)


def _w_in_prep_call(w_in, n_a):
    depth, d_model, n_in = w_in.shape
    assert n_in == n_a + N_IF + N_REST and n_a + LANES == W_IN_A
    assert N_REST % PREP_COLS_REST == 0 and N_REST % PREP_COLS_A == 0 and W_IN_A % PREP_COLS_A == 0
    w_t = jnp.swapaxes(w_in, 1, 2)
    out_shape = jax.ShapeDtypeStruct((depth, d_model, N_REST + W_IN_A), BF16)
    params = pltpu.CompilerParams(dimension_semantics=("arbitrary", "arbitrary"), vmem_limit_bytes=VMEM_LIMIT_BYTES)

    def window_spec(cols, first_row):
        return pl.BlockSpec((pl.Element(1), pl.Element(cols), pl.Element(d_model)),
                            lambda l, g: (l, (g * (cols // SUBLANES) + first_row // SUBLANES) * SUBLANES, 0))

    rest = pl.pallas_call(
        functools.partial(_w_in_prep_kernel, mask_step=None, keep_rows=None),
        grid=(depth, N_REST // PREP_COLS_REST),
        in_specs=[window_spec(PREP_COLS_REST, n_a + N_IF)],
        out_specs=pl.BlockSpec((None, d_model, PREP_COLS_REST), lambda l, g: (l, 0, g)),
        out_shape=out_shape, compiler_params=params, name="w_in_prep_rest",
    )(w_t)
    steps_a = W_IN_A // PREP_COLS_A
    return pl.pallas_call(
        functools.partial(_w_in_prep_kernel, mask_step=steps_a - 1, keep_rows=(n_a + N_IF) % PREP_COLS_A),
        grid=(depth, steps_a),
        in_specs=[window_spec(PREP_COLS_A, 0), pl.BlockSpec(memory_space=pl.ANY)],
        out_specs=pl.BlockSpec((None, d_model, PREP_COLS_A), lambda l, g: (l, 0, N_REST // PREP_COLS_A + g)),
        out_shape=out_shape, input_output_aliases={1: 0}, compiler_params=params, name="w_in_prep_a",
    )(w_t, rest)


def _mem_kv_kernel(mem_ref, g_ref, w_ref, o_ref):
    o_ref[...] = _bf(_dot(_bf(_rmsnorm(mem_ref[0], g_ref[...])), w_ref[...]))


def _mem_kv_call(mem, mem_norm_g, w_kv):
    bsz, m_len, d_model = mem.shape
    depth, _, n_kv = w_kv.shape
    return pl.pallas_call(
        _mem_kv_kernel,
        grid=(depth, bsz),
        in_specs=[pl.BlockSpec((1, m_len, d_model), lambda l, b: (b, 0, 0)),
                  pl.BlockSpec((None, 1, d_model), lambda l, b: (l, 0, 0)),
                  pl.BlockSpec((None, d_model, n_kv), lambda l, b: (l, 0, 0))],
        out_specs=pl.BlockSpec((None, None, m_len, n_kv), lambda l, b: (l, b, 0, 0)),
        out_shape=jax.ShapeDtypeStruct((depth, bsz, m_len, n_kv), BF16),
        compiler_params=pltpu.CompilerParams(dimension_semantics=("arbitrary", "arbitrary")),
        name="mem_kv",
    )(mem, mem_norm_g[:, None, :], w_kv)


def _layer_kernel(x_ref, pc_ref, pr_ref, kv_ref, ng_ref, win_ref, cwa_ref, bif_ref, mng_ref,
                  lcw_ref, lcb_ref, wlru_ref, lba_ref, lbx_ref, lam_ref, sink_ref, hlb_ref, hng_ref,
                  wbr_ref, wout_ref, fg_ref,
                  o_ref,
                  conv_a, conv_b, m_c, m_n, m_m, lru_h, swa_k, swa_v, swa_p, h_st,
                  *, layer, final, tq):
    c = pl.program_id(1)
    d_model = x_ref.shape[-1]

    @pl.when(c == 0)
    def _init():
        conv_a[...] = jnp.zeros(conv_a.shape, F32)
        conv_b[...] = jnp.zeros(conv_b.shape, F32)
        m_c[...] = jnp.zeros(m_c.shape, F32)
        m_n[...] = jnp.zeros(m_n.shape, F32)
        m_m[...] = jnp.full(m_m.shape, INIT_M, F32)
        lru_h[...] = jnp.zeros(lru_h.shape, F32)
        swa_k[...] = jnp.zeros(swa_k.shape, BF16)
        swa_v[...] = jnp.zeros(swa_v.shape, BF16)
        swa_p[...] = jnp.zeros(swa_p.shape, jnp.int32)
        h_st[...] = jnp.zeros(h_st.shape, F32)

    x = x_ref[0]
    hb = _bf(_rmsnorm(x, ng_ref[...]))

    gate_piece = d_model // 2
    n_gate_pieces = d_model // gate_piece
    widths = {"a_o": W_A, "a_z": W_A, "b_x": W_B, "b_z": W_B, "c_q": W_C, "c_kv": 2 * KV_W_C, "c_z": W_C,
              "d_q": W_D, "d_f": W_D, "d_i": W_D, "d_z": W_D, "e_q": W_E, "e_z": W_E}
    offsets = dict(_OFF, c_kv=_OFF["c_k"])
    gates = [("gate", j, hf) for j in range(N_BRANCH) for hf in range(n_gate_pieces)]
    pending = (["a_o", "a_z", "b_x", "b_z", "c_q", "c_kv"] + gates[:2] + ["c_z", "d_f", "d_q", "d_i"]
               + ["d_z", "e_q", "e_z"] + gates[2:])
    issued = {}

    def issue(piece):
        if isinstance(piece, tuple):
            _, j, hf = piece
            g0 = _OFF["gates"] + j * d_model + hf * gate_piece
            issued[piece] = jax.nn.sigmoid(_dot(hb, win_ref[:, g0:g0 + gate_piece]))
        else:
            o = offsets[piece]
            issued[piece] = _dot(hb, win_ref[:, o:o + widths[piece]])

    def issue_pending(n):
        for _ in range(min(n, len(pending))):
            issue(pending.pop(0))

    def proj(name, width):
        if name in widths:
            assert widths[name] == width
            if name not in issued:
                pending.remove(name)
                issue(name)
            return issued[name]
        o = _OFF[name]
        return _dot(hb, win_ref[:, o:o + width])

    branch_out = {}
    hd = LANES // 2

    def mixer_a():
        qk = _silu(_causal_conv(conv_a, proj("a_qk", 2 * W_A), cwa_ref, tq))
        yield
        v_a = proj("a_v", W_A)
        gif = proj("a_if", LANES) + bif_ref[...]
        logf = _log_sigmoid(gif)
        ri_a = lax.broadcasted_iota(jnp.int32, (L_A, L_A), 0)
        ci_a = lax.broadcasted_iota(jnp.int32, (L_A, L_A), 1)
        causal_a = ci_a <= ri_a
        tri_a = jnp.where(causal_a, 1.0, 0.0).astype(BF16)
        hm_chunks = []
        for n in range(tq // L_A):
            rs = slice(n * L_A, (n + 1) * L_A)
            gif_c = gif[rs]
            g_all = _cumsum_rows(tri_a, logf[rs])
            g_t = g_all.T
            gif_t = gif_c.T
            heads = []
            for h in range(N_HEAD_A):
                ls = slice(h * LANES, (h + 1) * LANES)
                hh, c_new, n_new, m_new = _mlstm_head(
                    qk[rs, ls] * (LANES ** -0.5), qk[rs, W_A + h * LANES:W_A + (h + 1) * LANES], v_a[rs, ls],
                    g_all[:, N_HEAD_A + h:N_HEAD_A + h + 1], gif_c[:, h:h + 1],
                    g_t[N_HEAD_A + h:N_HEAD_A + h + 1, :], gif_t[h:h + 1, :],
                    m_c[h], m_n[h, 0:1, :], m_m[h, 0:1, 0:1], causal_a)
                m_c[h] = c_new
                m_n[h] = jnp.broadcast_to(n_new, m_n.shape[1:])
                m_m[h] = jnp.broadcast_to(m_new, m_m.shape[1:])
                hh = hh * lax.rsqrt(jnp.mean(hh * hh, axis=-1, keepdims=True) + EPS)
                heads.append(hh)
                yield
            hm_chunks.append(jnp.concatenate(heads, axis=1))
        hm = jnp.concatenate(hm_chunks, axis=0) if len(hm_chunks) > 1 else hm_chunks[0]
        branch_out["a"] = hm * mng_ref[...] * jax.nn.sigmoid(proj("a_o", W_A)) * _silu(proj("a_z", W_A))

    def mixer_b():
        xc = _causal_conv(conv_b, proj("b_x", W_B), lcw_ref, tq) + lcb_ref[...]
        yield
        xcb = _bf(xc)
        half = W_B // 2
        rx0 = _dot(xcb[:, :half], wlru_ref[0])
        rx1 = _dot(xcb[:, half:], wlru_ref[1])
        r = jax.nn.sigmoid(jnp.concatenate([rx0[:, :half], rx1[:, :half]], axis=1) + lba_ref[...])
        i_g = jax.nn.sigmoid(jnp.concatenate([rx0[:, half:], rx1[:, half:]], axis=1) + lbx_ref[...])
        yield
        a_t = jnp.exp(-LRU_C * r * _softplus(-lam_ref[...]))
        b_t = jnp.sqrt(1.0 - a_t * a_t) * (i_g * xc)
        yield
        h_lru = yield from _linear_scan_rows(a_t, b_t, lru_h[0:1, :])
        lru_h[...] = jnp.broadcast_to(h_lru[tq - 1:tq, :], lru_h.shape)
        branch_out["b"] = h_lru * _silu(proj("b_z", W_B))

    def mixer_c():
        q_c = proj("c_q", W_C)
        kv_c = _bf(proj("c_kv", 2 * KV_W_C))
        k_c = kv_c[:, :KV_W_C]
        v_c = kv_c[:, KV_W_C:]
        pos_c = pc_ref[0]
        pos_r = pr_ref[0]
        rows2 = lax.broadcasted_iota(jnp.int32, (2 * BLK_C, 1), 0)
        ri_c = lax.broadcasted_iota(jnp.int32, (BLK_C, 2 * BLK_C), 0)
        ci_c = lax.broadcasted_iota(jnp.int32, (BLK_C, 2 * BLK_C), 1)
        rel = ri_c + BLK_C - ci_c
        in_window = (rel >= 0) & (rel < BLK_C)
        yc_blocks = []
        for n in range(tq // BLK_C):
            rs = slice(n * BLK_C, (n + 1) * BLK_C)
            k_prev = swa_k[...] if n == 0 else k_c[(n - 1) * BLK_C:n * BLK_C]
            v_prev = swa_v[...] if n == 0 else v_c[(n - 1) * BLK_C:n * BLK_C]
            p_prev = swa_p[0:1, :] if n == 0 else pos_r[:, (n - 1) * BLK_C:n * BLK_C]
            kk = jnp.concatenate([k_prev, k_c[rs]], axis=0)
            vv = jnp.concatenate([v_prev, v_c[rs]], axis=0)
            pk = jnp.concatenate([p_prev, pos_r[:, rs]], axis=1)
            dist = jnp.abs(pos_c[rs] - pk).astype(F32)
            first_key = jnp.where((c * (tq // BLK_C) + n) == 0, BLK_C, 0)
            valid = in_window & (ci_c >= first_key)
            dist2 = jnp.concatenate([dist, dist], axis=0)
            valid2 = jnp.concatenate([valid, valid], axis=0)
            kdup = [jnp.concatenate([kk[:, j * hd:(j + 1) * hd]] * 2, axis=1) for j in range(2)]
            vdup = [jnp.concatenate([vv[:, j * hd:(j + 1) * hd]] * 2, axis=1) for j in range(2)]
            pairs = []
            for p in range(N_HEAD_C // 2):
                kvh = (2 * p) // (N_HEAD_C // 2)
                top = rows2 < BLK_C
                slope = jnp.where(top, 2.0 ** (-(2 * p + 1)), 2.0 ** (-(2 * p + 2)))
                sink = jnp.where(top, sink_ref[0, 2 * p], sink_ref[0, 2 * p + 1])

                def bias(sc, slope=slope, dist2=dist2, valid2=valid2):
                    return jnp.where(valid2, sc - slope * dist2, -jnp.inf)

                pairs.append(_pair_attention(q_c[rs, p * LANES:(p + 1) * LANES], kdup[kvh], vdup[kvh],
                                             hd ** -0.5, bias, sink))
                yield
            yc_blocks.append(jnp.concatenate(pairs, axis=1))
        last = slice(tq - BLK_C, tq)
        swa_k[...] = k_c[last]
        swa_v[...] = v_c[last]
        swa_p[...] = jnp.broadcast_to(pos_r[:, last], swa_p.shape)
        yc = jnp.concatenate(yc_blocks, axis=0) if len(yc_blocks) > 1 else yc_blocks[0]
        branch_out["c"] = yc * _silu(proj("c_z", W_C))

    def mixer_d():
        lbp = hlb_ref[...]
        e_lb = jnp.exp(lbp - jnp.max(lbp, axis=0, keepdims=True))
        p_lb = e_lb / jnp.sum(e_lb, axis=0, keepdims=True)
        lb = jnp.zeros((1, W_D), F32)
        for j in range(1, layer + 1):
            lb = lb + p_lb[j:j + 1, :]
        ls_f = jnp.log1p(-lb) + _log_sigmoid(proj("d_f", W_D))
        log_lb = jnp.log(lb)
        mx = jnp.maximum(log_lb, ls_f)
        log_f = mx + jnp.log(1.0 + jnp.exp(-jnp.abs(log_lb - ls_f)))
        k_d = 1.0 - jnp.exp(log_f)
        yield
        ri_d = lax.broadcasted_iota(jnp.int32, (tq, tq), 0)
        ci_d = lax.broadcasted_iota(jnp.int32, (tq, tq), 1)
        tri_d = jnp.where(ci_d <= ri_d, 1.0, 0.0).astype(BF16)
        b_all = _cumsum_rows(tri_d, log_f)
        qs_d = _silu(proj("d_q", W_D)) * (LANES ** -0.5)
        v_d = proj("d_i", W_D)
        yield
        hd_heads = []
        for h in range(N_HEAD_D):
            ls = slice(h * LANES, (h + 1) * LANES)
            o_h, st_new = yield from _hgrn_head(qs_d[:, ls], k_d[:, ls], v_d[:, ls], b_all[:, ls], h_st[h],
                                                ri_d, ci_d)
            h_st[h] = st_new
            hd_heads.append(o_h * lax.rsqrt(jnp.mean(o_h * o_h, axis=-1, keepdims=True) + EPS))
            yield
        branch_out["d"] = jnp.concatenate(hd_heads, axis=1) * hng_ref[...] * _silu(proj("d_z", W_D))

    def mixer_e():
        q_e = proj("e_q", W_E)
        pairs = []
        for p in range(W_E // LANES):
            ls = slice(p * LANES, (p + 1) * LANES)
            pairs.append(_pair_attention(q_e[:, ls], kv_ref[:, ls], kv_ref[:, W_E + p * LANES:W_E + (p + 1) * LANES],
                                         hd ** -0.5, lambda sc: sc, None))
            yield
        branch_out["e"] = jnp.concatenate(pairs, axis=1) * _silu(proj("e_z", W_E))

    yields_per_head_d = (tq // 2 // DIAG_D).bit_length() + 2
    heads_d = [0, 0] + ([1] + [0] * (yields_per_head_d - 1)) * N_HEAD_D
    _run_with_fill([[(mixer_a(), [0] + [1] * (N_HEAD_A * (tq // L_A)))],
                    [(mixer_b(), [1, 1, 1, 1])],
                    [(mixer_c(), [1] * (N_HEAD_C // 2 * (tq // BLK_C)))],
                    [(mixer_d(), heads_d)],
                    [(mixer_e(), [])]], issue_pending)

    issue_pending(len(pending))
    y_pieces = []
    for hf in range(n_gate_pieces):
        cs = slice(hf * gate_piece, (hf + 1) * gate_piece)
        y = None
        row0 = 0
        for j, (name, wj) in enumerate(zip("abcde", BRANCH_W)):
            t_j = issued[("gate", j, hf)] * _dot(_bf(branch_out[name]), wbr_ref[row0:row0 + wj, cs])
            y = t_j if y is None else y + t_j
            row0 += wj
        y_pieces.append(_bf(y))
    x_new = x + _dot(jnp.concatenate(y_pieces, axis=1), wout_ref[...])
    if final:
        x_new = _rmsnorm(x_new, fg_ref[...])
    o_ref[0] = x_new


def _const_spec(shape):
    nd = len(shape)
    return pl.BlockSpec(shape, lambda b, c: (0,) * nd, pipeline_mode=pl.Buffered(1))


_STACKED = ("w_in", "w_lru", "w_br", "w_out")


def _layer_spec(shape, layer):
    nd = len(shape)
    return pl.BlockSpec((None,) + tuple(shape[1:]), lambda b, c: (layer,) + (0,) * (nd - 1),
                        pipeline_mode=pl.Buffered(1))


def _layer_call(x, pos_col, pos_row, mem_kv, params, *, layer, final, tq):
    bsz, seq, d_model = x.shape
    in_specs = [
        pl.BlockSpec((1, tq, d_model), lambda b, c: (b, c, 0)),
        pl.BlockSpec((1, tq, 1), lambda b, c: (b, c, 0)),
        pl.BlockSpec((1, 1, tq), lambda b, c: (b, 0, c)),
        pl.BlockSpec((None, None) + tuple(mem_kv.shape[2:]), lambda b, c: (layer, b, 0, 0)),
    ]
    for name, p in params:
        if name == "sinks":
            in_specs.append(pl.BlockSpec(memory_space=pltpu.SMEM))
        elif name in _STACKED:
            in_specs.append(_layer_spec(p.shape, layer))
        else:
            in_specs.append(_const_spec(p.shape))
    scratch = [
        pltpu.VMEM((SUBLANES, 2 * W_A), F32),
        pltpu.VMEM((SUBLANES, W_B), F32),
        pltpu.VMEM((N_HEAD_A, LANES, LANES), F32),
        pltpu.VMEM((N_HEAD_A, SUBLANES, LANES), F32),
        pltpu.VMEM((N_HEAD_A, SUBLANES, LANES), F32),
        pltpu.VMEM((SUBLANES, W_B), F32),
        pltpu.VMEM((BLK_C, KV_W_C), BF16),
        pltpu.VMEM((BLK_C, KV_W_C), BF16),
        pltpu.VMEM((SUBLANES, BLK_C), jnp.int32),
        pltpu.VMEM((N_HEAD_D, LANES, LANES), F32),
    ]
    return pl.pallas_call(
        functools.partial(_layer_kernel, layer=layer, final=final, tq=tq),
        grid=(bsz, seq // tq),
        in_specs=in_specs,
        out_specs=pl.BlockSpec((1, tq, d_model), lambda b, c: (b, c, 0)),
        out_shape=jax.ShapeDtypeStruct(x.shape, x.dtype),
        scratch_shapes=scratch,
        compiler_params=pltpu.CompilerParams(dimension_semantics=("arbitrary", "arbitrary"),
                                             vmem_limit_bytes=VMEM_LIMIT_BYTES),
        name=f"hybrid_layer_{layer}",
    )(x, pos_col, pos_row, mem_kv, *[p for _, p in params])


def _tile_rows(seq):
    for t in (256, 128):
        if seq % t == 0:
            return t
    raise ValueError("sequence length must be a multiple of 128")


def kernel(x, mem, positions, norm_g, w_in, mlstm_conv_w, mlstm_b_if, mlstm_norm_g, lru_conv_w, lru_conv_b, lru_wa, lru_ba, lru_wx, lru_bx, lru_lambda, swa_sinks, hgrn_lb, hgrn_norm_g, mem_norm_g, w_mem_kv, w_br, w_out, final_norm_g):
    depth, d_model, n_in = w_in.shape
    n_a = 5 * W_A
    assert d_model == D_MODEL and n_in == n_a + N_IF + N_REST
    tq = _tile_rows(x.shape[1])

    w_in_p = _w_in_prep_call(w_in, n_a)
    b_if = jnp.pad(mlstm_b_if, ((0, 0), (0, LANES - N_IF)))[:, None, :]
    nb, bd = lru_wa.shape[1], lru_wa.shape[2]
    eye = jnp.eye(nb, dtype=lru_wa.dtype)

    def block_diag(w):
        return (eye[None, :, None, :, None] * w[:, :, :, None, :]).reshape(depth, nb * bd, nb * bd)

    wa_d, wx_d = block_diag(lru_wa), block_diag(lru_wx)
    hw = W_B // 2
    w_lru = jnp.stack([jnp.concatenate([wa_d[:, :hw, :hw], wx_d[:, :hw, :hw]], axis=-1),
                       jnp.concatenate([wa_d[:, hw:, hw:], wx_d[:, hw:, hw:]], axis=-1)], axis=1).astype(BF16)
    mem_kv = _mem_kv_call(mem, mem_norm_g, w_mem_kv.astype(BF16))
    w_br_b = w_br.astype(BF16)
    w_out_b = w_out.astype(BF16)

    pos_col = positions[:, :, None]
    pos_row = positions[:, None, :]
    row = lambda a, l: a[l][None, :]
    for l in range(depth):
        params = [
            ("norm_g", row(norm_g, l)), ("w_in", w_in_p), ("conv_a", mlstm_conv_w[l]), ("b_if", b_if[l]),
            ("mlstm_norm_g", row(mlstm_norm_g, l)), ("lru_conv_w", lru_conv_w[l]), ("lru_conv_b", row(lru_conv_b, l)),
            ("w_lru", w_lru), ("lru_ba", row(lru_ba, l)), ("lru_bx", row(lru_bx, l)), ("lru_lambda", row(lru_lambda, l)),
            ("sinks", row(swa_sinks, l)), ("hgrn_lb", hgrn_lb), ("hgrn_norm_g", row(hgrn_norm_g, l)),
            ("w_br", w_br_b), ("w_out", w_out_b), ("final_norm_g", final_norm_g[None, :]),
        ]
        x = _layer_call(x, pos_col, pos_row, mem_kv, params, layer=l, final=(l == depth - 1), tq=tq)
    return x
```

```python
import functools
import math

import jax
import jax.numpy as jnp
from jax import lax
from jax.experimental import pallas as pl
from jax.experimental.pallas import tpu as pltpu

F32 = jnp.float32
BF16 = jnp.bfloat16

EPS = 1e-6
CONV_W = 4
N_HEAD_A = 4
W_A = 512
L_A = 128
INIT_M = -1e30
W_B = 512
LRU_C = 8.0
N_HEAD_C = 8
W_C = 512
KV_W_C = 128
BLK_C = 128
N_HEAD_D = 4
W_D = 512
DIAG_D = 8
W_E = 256
N_BRANCH = 5
BRANCH_W = (W_A, W_B, W_C, W_D, W_E)

LANES = 128
SUBLANES = 8
VMEM_LIMIT_BYTES = 58 * 1024 * 1024

D_MODEL = 1024
N_IF = 2 * N_HEAD_A

_OFF = {}
_acc = 0
for _name, _w in (("b_x", W_B), ("b_z", W_B),
                  ("c_q", W_C), ("c_k", KV_W_C), ("c_v", KV_W_C), ("c_z", W_C),
                  ("d_q", W_D), ("d_f", W_D), ("d_i", W_D), ("d_z", W_D),
                  ("e_q", W_E), ("e_z", W_E), ("gates", N_BRANCH * D_MODEL),
                  ("a_qk", 2 * W_A), ("a_v", W_A), ("a_o", W_A), ("a_z", W_A), ("a_if", LANES)):
    _OFF[_name] = _acc
    _acc += _w
N_REST = _OFF["a_qk"]
W_IN_A = _acc - N_REST


def _bf(x):
    return x.astype(BF16)


def _dot(a, b):
    return jnp.dot(a, b, preferred_element_type=F32)


def _dot_nt(a, b):
    return lax.dot_general(a, b, (((1,), (1,)), ((), ())), preferred_element_type=F32)


def _dot_tn(a, b):
    return lax.dot_general(a, b, (((0,), (0,)), ((), ())), preferred_element_type=F32)


def _silu(x):
    return x * jax.nn.sigmoid(x)


def _log_sigmoid(x):
    return jnp.minimum(x, 0.0) - jnp.log(1.0 + jnp.exp(-jnp.abs(x)))


def _softplus(x):
    return jnp.maximum(x, 0.0) + jnp.log(1.0 + jnp.exp(-jnp.abs(x)))


def _linear_scan_rows(a, b, h0):
    t, w = a.shape
    groups = t // SUBLANES
    a3 = a.reshape(groups, SUBLANES, w)
    b3 = b.reshape(groups, SUBLANES, w)
    sub = lax.broadcasted_iota(jnp.int32, a3.shape, 1)
    s = 1
    while s < SUBLANES:
        keep = sub >= s
        a_s = jnp.where(keep, pltpu.roll(a3, s, 1), 1.0)
        b_s = jnp.where(keep, pltpu.roll(b3, s, 1), 0.0)
        b3 = a3 * b_s + b3
        a3 = a3 * a_s
        s *= 2
    yield
    carry = h0
    out = []
    for g in range(groups):
        hg = b3[g] + a3[g] * carry
        out.append(hg)
        carry = hg[SUBLANES - 1:SUBLANES, :]
        if g % SUBLANES == SUBLANES - 1:
            yield
    return jnp.concatenate(out, axis=0)


def _rmsnorm(x, g):
    return x * lax.rsqrt(jnp.mean(x * x, axis=-1, keepdims=True) + EPS) * g


def _cumsum_rows(tri, x):
    hi = _bf(x)
    r1 = x - hi.astype(F32)
    mid = _bf(r1)
    lo = _bf(r1 - mid.astype(F32))
    return _dot(tri, hi) + _dot(tri, mid) + _dot(tri, lo)


def _block_row_bcast(b, blk, idx):
    n = b.shape[0] // blk
    b3 = b.reshape(n, blk, b.shape[1])
    return jnp.broadcast_to(b3[:, idx:idx + 1, :], b3.shape).reshape(b.shape)


def _causal_conv(tail, x, w_ref, t):
    groups = t // SUBLANES
    x3 = x.reshape(groups, SUBLANES, x.shape[1])
    sub = lax.broadcasted_iota(jnp.int32, x3.shape, 1)
    tail3 = tail[...][None]
    y = w_ref[CONV_W - 1:CONV_W, :] * x3
    for s in range(1, CONV_W):
        rolled = pltpu.roll(x3, s, 1)
        rolled_prev = jnp.concatenate([pltpu.roll(tail3, s, 1), rolled[:-1]], axis=0)
        y = y + w_ref[CONV_W - 1 - s:CONV_W - s, :] * jnp.where(sub >= s, rolled, rolled_prev)
    tail[...] = x3[groups - 1]
    return y.reshape(x.shape)


def _mlstm_head(q, k, v, g, icol, grow, irow, c_st, n_st, m_st, causal):
    L = q.shape[0]
    dm = jnp.where(causal, g - grow + irow, -jnp.inf)
    g_end = g[L - 1:L, :]
    w_end = g_end - g + icol
    m_loc = jnp.max(w_end, axis=0, keepdims=True)
    ke = k * jnp.exp(w_end - m_loc)
    vb = _bf(v)
    c_loc = _dot_tn(_bf(ke), vb)
    n_loc = jnp.sum(ke, axis=0, keepdims=True)

    a_inter = g + m_st
    m_j = jnp.maximum(a_inter, jnp.max(dm, axis=-1, keepdims=True))
    qb = _bf(q)
    p = jnp.exp(dm - m_j) * _dot_nt(qb, _bf(k))
    w_inter = jnp.exp(a_inter - m_j)
    num = _dot(_bf(p), vb) + w_inter * _dot(qb, _bf(c_st))
    den = jnp.sum(p, axis=-1, keepdims=True) + w_inter * jnp.sum(q * n_st, axis=-1, keepdims=True)
    h = num / jnp.maximum(jnp.abs(den), jnp.exp(-m_j))

    m_new = jnp.maximum(g_end + m_st, m_loc)
    a = jnp.exp(g_end + m_st - m_new)
    b = jnp.exp(m_loc - m_new)
    return h, a * c_st + b * c_loc, a * n_st + b * n_loc, m_new


def _hgrn_head(qs, kk, v, b, st_t, ri, ci):
    t = qs.shape[0]
    rows = ri[:, 0:1]
    o = _dot_nt(_bf(qs * jnp.exp(b)), _bf(st_t))
    b_end = b[t - 1:t, :]
    vb = _bf(v)
    st_new = st_t * jnp.exp(b_end) + _dot_tn(vb, _bf(kk * jnp.exp(b_end - b)))
    yield

    amat = None
    s = t // 2
    while s >= DIAG_D:
        sh = s.bit_length() - 1
        e = jnp.exp(-jnp.abs(b - _block_row_bcast(b, 2 * s, s - 1)))
        upper = (rows & (2 * s - 1)) >= s
        xl = _bf(jnp.where(upper, qs, kk) * e)
        al = _dot_nt(xl, xl)
        bj = ri >> sh
        wanted = (((bj - (ci >> sh)) << 2) + (bj & 1)) == 5
        amat = jnp.where(wanted, al, 0.0 if amat is None else amat)
        s //= 2
        yield
    sh = DIAG_D.bit_length() - 1
    d = b - _block_row_bcast(b, DIAG_D, 0)
    ad = _dot_nt(_bf(qs * jnp.exp(d)), _bf(kk * jnp.exp(-d)))
    diag = ((ri >> sh) == (ci >> sh)) & (ci <= ri)
    amat = jnp.where(diag, ad, amat)
    return o + _dot(_bf(amat), vb), st_new


def _pair_attention(qp, kmat, vmat, scale, bias_fn, sink_col):
    t = qp.shape[0]
    lane = lax.broadcasted_iota(jnp.int32, qp.shape, 1)
    lo = lane < (LANES // 2)
    assert math.log2(scale).is_integer()
    qs = qp * scale
    q2 = jnp.concatenate([jnp.where(lo, qs, 0.0), jnp.where(lo, 0.0, qs)], axis=0)
    s = bias_fn(_dot_nt(_bf(q2), kmat))
    m = jnp.max(s, axis=-1, keepdims=True)
    if sink_col is not None:
        m = jnp.maximum(m, sink_col)
    p = jnp.exp(s - m)
    den = jnp.sum(p, axis=-1, keepdims=True)
    if sink_col is not None:
        den = den + jnp.exp(sink_col - m)
    o2 = _dot(_bf(p), vmat) / den
    return jnp.where(lo, o2[:t], o2[t:])


def _run_with_fill(tasks, issue):
    for group in tasks:
        active = [(gen, pieces, [0]) for gen, pieces in group]
        while active:
            for task in list(active):
                gen, pieces, count = task
                try:
                    next(gen)
                except StopIteration:
                    active.remove(task)
                    continue
                if count[0] < len(pieces):
                    issue(pieces[count[0]])
                count[0] += 1


PREP_COLS_A = 384
PREP_COLS_REST = 1664


def _w_in_prep_kernel(wt_ref, *refs, mask_step, keep_rows):
    t = wt_ref[0]
    if mask_step is not None:
        row = lax.broadcasted_iota(jnp.int32, t.shape, 0)
        t = jnp.where((pl.program_id(1) != mask_step) | (row < keep_rows), t, 0.0)
    refs[-1][...] = _bf(t.T)


def _w_in_prep_call(w_in, n_a):
    depth, d_model, n_in = w_in.shape
    assert n_in == n_a + N_IF + N_REST and n_a + LANES == W_IN_A
    assert N_REST % PREP_COLS_REST == 0 and W_IN_A % PREP_COLS_A == 0
    w_t = jnp.swapaxes(w_in, 1, 2)
    params = pltpu.CompilerParams(dimension_semantics=("arbitrary", "arbitrary"), vmem_limit_bytes=VMEM_LIMIT_BYTES)

    def window_spec(cols, first_row):
        return pl.BlockSpec((pl.Element(1), pl.Element(cols), pl.Element(d_model)),
                            lambda l, g: (l, (g * (cols // SUBLANES) + first_row // SUBLANES) * SUBLANES, 0))

    rest = pl.pallas_call(
        functools.partial(_w_in_prep_kernel, mask_step=None, keep_rows=None),
        grid=(depth, N_REST // PREP_COLS_REST),
        in_specs=[window_spec(PREP_COLS_REST, n_a + N_IF)],
        out_specs=pl.BlockSpec((None, d_model, PREP_COLS_REST), lambda l, g: (l, 0, g)),
        out_shape=jax.ShapeDtypeStruct((depth, d_model, N_REST), BF16), compiler_params=params,
        name="w_in_prep_rest",
    )(w_t)
    steps_a = W_IN_A // PREP_COLS_A
    mlstm = pl.pallas_call(
        functools.partial(_w_in_prep_kernel, mask_step=steps_a - 1, keep_rows=(n_a + N_IF) % PREP_COLS_A),
        grid=(depth, steps_a),
        in_specs=[window_spec(PREP_COLS_A, 0)],
        out_specs=pl.BlockSpec((None, d_model, PREP_COLS_A), lambda l, g: (l, 0, g)),
        out_shape=jax.ShapeDtypeStruct((depth, d_model, W_IN_A), BF16), compiler_params=params,
        name="w_in_prep_a",
    )(w_t)
    return rest, mlstm


def _mem_kv_kernel(mem_ref, g_ref, w_ref, o_ref):
    o_ref[...] = _bf(_dot(_bf(_rmsnorm(mem_ref[0], g_ref[...])), w_ref[...]))


def _mem_kv_call(mem, mem_norm_g, w_kv):
    bsz, m_len, d_model = mem.shape
    depth, _, n_kv = w_kv.shape
    return pl.pallas_call(
        _mem_kv_kernel,
        grid=(depth, bsz),
        in_specs=[pl.BlockSpec((1, m_len, d_model), lambda l, b: (b, 0, 0)),
                  pl.BlockSpec((None, 1, d_model), lambda l, b: (l, 0, 0)),
                  pl.BlockSpec((None, d_model, n_kv), lambda l, b: (l, 0, 0))],
        out_specs=pl.BlockSpec((None, None, m_len, n_kv), lambda l, b: (l, b, 0, 0)),
        out_shape=jax.ShapeDtypeStruct((depth, bsz, m_len, n_kv), BF16),
        compiler_params=pltpu.CompilerParams(dimension_semantics=("arbitrary", "arbitrary")),
        name="mem_kv",
    )(mem, mem_norm_g[:, None, :], w_kv)


def _layer_kernel(x_ref, pc_ref, pr_ref, kv_ref, ng_ref, win_ref, wina_ref, cwa_ref, bif_ref, mng_ref,
                  lcw_ref, lcb_ref, wlru_ref, lba_ref, lbx_ref, lam_ref, sink_ref, hlb_ref, hng_ref,
                  wbr_ref, wout_ref, fg_ref,
                  o_ref,
                  conv_a, conv_b, m_c, m_n, m_m, lru_h, swa_k, swa_v, swa_p, h_st,
                  *, layer, final, tq):
    c = pl.program_id(1)
    d_model = x_ref.shape[-1]

    @pl.when(c == 0)
    def _init():
        conv_a[...] = jnp.zeros(conv_a.shape, F32)
        conv_b[...] = jnp.zeros(conv_b.shape, F32)
        m_c[...] = jnp.zeros(m_c.shape, F32)
        m_n[...] = jnp.zeros(m_n.shape, F32)
        m_m[...] = jnp.full(m_m.shape, INIT_M, F32)
        lru_h[...] = jnp.zeros(lru_h.shape, F32)
        swa_k[...] = jnp.zeros(swa_k.shape, BF16)
        swa_v[...] = jnp.zeros(swa_v.shape, BF16)
        swa_p[...] = jnp.zeros(swa_p.shape, jnp.int32)
        h_st[...] = jnp.zeros(h_st.shape, F32)

    x = x_ref[0]
    hb = _bf(_rmsnorm(x, ng_ref[...]))

    def w_cols(o, width):
        if o >= N_REST:
            return wina_ref[:, o - N_REST:o - N_REST + width]
        return win_ref[:, o:o + width]

    gate_piece = d_model // 2
    n_gate_pieces = d_model // gate_piece
    widths = {"a_o": W_A, "a_z": W_A, "b_x": W_B, "b_z": W_B, "c_q": W_C, "c_kv": 2 * KV_W_C, "c_z": W_C,
              "d_q": W_D, "d_f": W_D, "d_i": W_D, "d_z": W_D, "e_q": W_E, "e_z": W_E}
    offsets = dict(_OFF, c_kv=_OFF["c_k"])
    gates = [("gate", j, hf) for j in range(N_BRANCH) for hf in range(n_gate_pieces)]
    pending = (["a_o", "a_z", "b_x", "b_z", "c_q", "c_kv"] + gates[:2] + ["c_z", "d_f", "d_q", "d_i"]
               + ["d_z", "e_q", "e_z"] + gates[2:])
    issued = {}

    def issue(piece):
        if isinstance(piece, tuple):
            _, j, hf = piece
            g0 = _OFF["gates"] + j * d_model + hf * gate_piece
            issued[piece] = jax.nn.sigmoid(_dot(hb, w_cols(g0, gate_piece)))
        else:
            o = offsets[piece]
            issued[piece] = _dot(hb, w_cols(o, widths[piece]))

    def issue_pending(n):
        for _ in range(min(n, len(pending))):
            issue(pending.pop(0))

    def proj(name, width):
        if name in widths:
            assert widths[name] == width
            if name not in issued:
                pending.remove(name)
                issue(name)
            return issued[name]
        o = _OFF[name]
        return _dot(hb, w_cols(o, width))

    branch_out = {}
    hd = LANES // 2

    def mixer_a():
        qk = _silu(_causal_conv(conv_a, proj("a_qk", 2 * W_A), cwa_ref, tq))
        yield
        v_a = proj("a_v", W_A)
        gif = proj("a_if", LANES) + bif_ref[...]
        logf = _log_sigmoid(gif)
        ri_a = lax.broadcasted_iota(jnp.int32, (L_A, L_A), 0)
        ci_a = lax.broadcasted_iota(jnp.int32, (L_A, L_A), 1)
        causal_a = ci_a <= ri_a
        tri_a = jnp.where(causal_a, 1.0, 0.0).astype(BF16)
        hm_chunks = []
        for n in range(tq // L_A):
            rs = slice(n * L_A, (n + 1) * L_A)
            gif_c = gif[rs]
            g_all = _cumsum_rows(tri_a, logf[rs])
            g_t = g_all.T
            gif_t = gif_c.T
            heads = []
            for h in range(N_HEAD_A):
                ls = slice(h * LANES, (h + 1) * LANES)
                hh, c_new, n_new, m_new = _mlstm_head(
                    qk[rs, ls] * (LANES ** -0.5), qk[rs, W_A + h * LANES:W_A + (h + 1) * LANES], v_a[rs, ls],
                    g_all[:, N_HEAD_A + h:N_HEAD_A + h + 1], gif_c[:, h:h + 1],
                    g_t[N_HEAD_A + h:N_HEAD_A + h + 1, :], gif_t[h:h + 1, :],
                    m_c[h], m_n[h, 0:1, :], m_m[h, 0:1, 0:1], causal_a)
                m_c[h] = c_new
                m_n[h] = jnp.broadcast_to(n_new, m_n.shape[1:])
                m_m[h] = jnp.broadcast_to(m_new, m_m.shape[1:])
                hh = hh * lax.rsqrt(jnp.mean(hh * hh, axis=-1, keepdims=True) + EPS)
                heads.append(hh)
                yield
            hm_chunks.append(jnp.concatenate(heads, axis=1))
        hm = jnp.concatenate(hm_chunks, axis=0) if len(hm_chunks) > 1 else hm_chunks[0]
        branch_out["a"] = hm * mng_ref[...] * jax.nn.sigmoid(proj("a_o", W_A)) * _silu(proj("a_z", W_A))

    def mixer_b():
        xc = _causal_conv(conv_b, proj("b_x", W_B), lcw_ref, tq) + lcb_ref[...]
        yield
        xcb = _bf(xc)
        half = W_B // 2
        rx0 = _dot(xcb[:, :half], wlru_ref[0])
        rx1 = _dot(xcb[:, half:], wlru_ref[1])
        r = jax.nn.sigmoid(jnp.concatenate([rx0[:, :half], rx1[:, :half]], axis=1) + lba_ref[...])
        i_g = jax.nn.sigmoid(jnp.concatenate([rx0[:, half:], rx1[:, half:]], axis=1) + lbx_ref[...])
        yield
        a_t = jnp.exp(-LRU_C * r * _softplus(-lam_ref[...]))
        b_t = jnp.sqrt(1.0 - a_t * a_t) * (i_g * xc)
        yield
        h_lru = yield from _linear_scan_rows(a_t, b_t, lru_h[0:1, :])
        lru_h[...] = jnp.broadcast_to(h_lru[tq - 1:tq, :], lru_h.shape)
        branch_out["b"] = h_lru * _silu(proj("b_z", W_B))

    def mixer_c():
        q_c = proj("c_q", W_C)
        kv_c = _bf(proj("c_kv", 2 * KV_W_C))
        k_c = kv_c[:, :KV_W_C]
        v_c = kv_c[:, KV_W_C:]
        pos_c = pc_ref[0]
        pos_r = pr_ref[0]
        rows2 = lax.broadcasted_iota(jnp.int32, (2 * BLK_C, 1), 0)
        ri_c = lax.broadcasted_iota(jnp.int32, (BLK_C, 2 * BLK_C), 0)
        ci_c = lax.broadcasted_iota(jnp.int32, (BLK_C, 2 * BLK_C), 1)
        rel = ri_c + BLK_C - ci_c
        in_window = (rel >= 0) & (rel < BLK_C)
        yc_blocks = []
        for n in range(tq // BLK_C):
            rs = slice(n * BLK_C, (n + 1) * BLK_C)
            k_prev = swa_k[...] if n == 0 else k_c[(n - 1) * BLK_C:n * BLK_C]
            v_prev = swa_v[...] if n == 0 else v_c[(n - 1) * BLK_C:n * BLK_C]
            p_prev = swa_p[0:1, :] if n == 0 else pos_r[:, (n - 1) * BLK_C:n * BLK_C]
            kk = jnp.concatenate([k_prev, k_c[rs]], axis=0)
            vv = jnp.concatenate([v_prev, v_c[rs]], axis=0)
            pk = jnp.concatenate([p_prev, pos_r[:, rs]], axis=1)
            dist = jnp.abs(pos_c[rs] - pk).astype(F32)
            first_key = jnp.where((c * (tq // BLK_C) + n) == 0, BLK_C, 0)
            valid = in_window & (ci_c >= first_key)
            dist2 = jnp.concatenate([dist, dist], axis=0)
            valid2 = jnp.concatenate([valid, valid], axis=0)
            kdup = [jnp.concatenate([kk[:, j * hd:(j + 1) * hd]] * 2, axis=1) for j in range(2)]
            vdup = [jnp.concatenate([vv[:, j * hd:(j + 1) * hd]] * 2, axis=1) for j in range(2)]
            pairs = []
            for p in range(N_HEAD_C // 2):
                kvh = (2 * p) // (N_HEAD_C // 2)
                top = rows2 < BLK_C
                slope = jnp.where(top, 2.0 ** (-(2 * p + 1)), 2.0 ** (-(2 * p + 2)))
                sink = jnp.where(top, sink_ref[0, 2 * p], sink_ref[0, 2 * p + 1])

                def bias(sc, slope=slope, dist2=dist2, valid2=valid2):
                    return jnp.where(valid2, sc - slope * dist2, -jnp.inf)

                pairs.append(_pair_attention(q_c[rs, p * LANES:(p + 1) * LANES], kdup[kvh], vdup[kvh],
                                             hd ** -0.5, bias, sink))
                yield
            yc_blocks.append(jnp.concatenate(pairs, axis=1))
        last = slice(tq - BLK_C, tq)
        swa_k[...] = k_c[last]
        swa_v[...] = v_c[last]
        swa_p[...] = jnp.broadcast_to(pos_r[:, last], swa_p.shape)
        yc = jnp.concatenate(yc_blocks, axis=0) if len(yc_blocks) > 1 else yc_blocks[0]
        branch_out["c"] = yc * _silu(proj("c_z", W_C))

    def mixer_d():
        lbp = hlb_ref[...]
        e_lb = jnp.exp(lbp - jnp.max(lbp, axis=0, keepdims=True))
        p_lb = e_lb / jnp.sum(e_lb, axis=0, keepdims=True)
        lb = jnp.zeros((1, W_D), F32)
        for j in range(1, layer + 1):
            lb = lb + p_lb[j:j + 1, :]
        ls_f = jnp.log1p(-lb) + _log_sigmoid(proj("d_f", W_D))
        log_lb = jnp.log(lb)
        mx = jnp.maximum(log_lb, ls_f)
        log_f = mx + jnp.log(1.0 + jnp.exp(-jnp.abs(log_lb - ls_f)))
        k_d = 1.0 - jnp.exp(log_f)
        yield
        ri_d = lax.broadcasted_iota(jnp.int32, (tq, tq), 0)
        ci_d = lax.broadcasted_iota(jnp.int32, (tq, tq), 1)
        tri_d = jnp.where(ci_d <= ri_d, 1.0, 0.0).astype(BF16)
        b_all = _cumsum_rows(tri_d, log_f)
        qs_d = _silu(proj("d_q", W_D)) * (LANES ** -0.5)
        v_d = proj("d_i", W_D)
        yield
        hd_heads = []
        for h in range(N_HEAD_D):
            ls = slice(h * LANES, (h + 1) * LANES)
            o_h, st_new = yield from _hgrn_head(qs_d[:, ls], k_d[:, ls], v_d[:, ls], b_all[:, ls], h_st[h],
                                                ri_d, ci_d)
            h_st[h] = st_new
            hd_heads.append(o_h * lax.rsqrt(jnp.mean(o_h * o_h, axis=-1, keepdims=True) + EPS))
            yield
        branch_out["d"] = jnp.concatenate(hd_heads, axis=1) * hng_ref[...] * _silu(proj("d_z", W_D))

    def mixer_e():
        q_e = proj("e_q", W_E)
        pairs = []
        for p in range(W_E // LANES):
            ls = slice(p * LANES, (p + 1) * LANES)
            pairs.append(_pair_attention(q_e[:, ls], kv_ref[:, ls], kv_ref[:, W_E + p * LANES:W_E + (p + 1) * LANES],
                                         hd ** -0.5, lambda sc: sc, None))
            yield
        branch_out["e"] = jnp.concatenate(pairs, axis=1) * _silu(proj("e_z", W_E))

    yields_per_head_d = (tq // 2 // DIAG_D).bit_length() + 2
    heads_d = [0, 0] + ([1] + [0] * (yields_per_head_d - 1)) * N_HEAD_D
    _run_with_fill([[(mixer_a(), [0] + [1] * (N_HEAD_A * (tq // L_A)))],
                    [(mixer_b(), [1, 1, 1, 1])],
                    [(mixer_c(), [1] * (N_HEAD_C // 2 * (tq // BLK_C)))],
                    [(mixer_d(), heads_d)],
                    [(mixer_e(), [])]], issue_pending)

    issue_pending(len(pending))
    y_pieces = []
    for hf in range(n_gate_pieces):
        cs = slice(hf * gate_piece, (hf + 1) * gate_piece)
        y = None
        row0 = 0
        for j, (name, wj) in enumerate(zip("abcde", BRANCH_W)):
            t_j = issued[("gate", j, hf)] * _dot(_bf(branch_out[name]), wbr_ref[row0:row0 + wj, cs])
            y = t_j if y is None else y + t_j
            row0 += wj
        y_pieces.append(_bf(y))
    x_new = x + _dot(jnp.concatenate(y_pieces, axis=1), wout_ref[...])
    if final:
        x_new = _rmsnorm(x_new, fg_ref[...])
    o_ref[0] = x_new


def _const_spec(shape):
    nd = len(shape)
    return pl.BlockSpec(shape, lambda b, c: (0,) * nd, pipeline_mode=pl.Buffered(1))


_STACKED = ("w_in", "w_in_a", "w_lru", "w_br", "w_out")


def _layer_spec(shape, layer):
    nd = len(shape)
    return pl.BlockSpec((None,) + tuple(shape[1:]), lambda b, c: (layer,) + (0,) * (nd - 1),
                        pipeline_mode=pl.Buffered(1))


def _layer_call(x, pos_col, pos_row, mem_kv, params, *, layer, final, tq):
    bsz, seq, d_model = x.shape
    in_specs = [
        pl.BlockSpec((1, tq, d_model), lambda b, c: (b, c, 0)),
        pl.BlockSpec((1, tq, 1), lambda b, c: (b, c, 0)),
        pl.BlockSpec((1, 1, tq), lambda b, c: (b, 0, c)),
        pl.BlockSpec((None, None) + tuple(mem_kv.shape[2:]), lambda b, c: (layer, b, 0, 0)),
    ]
    for name, p in params:
        if name == "sinks":
            in_specs.append(pl.BlockSpec(memory_space=pltpu.SMEM))
        elif name in _STACKED:
            in_specs.append(_layer_spec(p.shape, layer))
        else:
            in_specs.append(_const_spec(p.shape))
    scratch = [
        pltpu.VMEM((SUBLANES, 2 * W_A), F32),
        pltpu.VMEM((SUBLANES, W_B), F32),
        pltpu.VMEM((N_HEAD_A, LANES, LANES), F32),
        pltpu.VMEM((N_HEAD_A, SUBLANES, LANES), F32),
        pltpu.VMEM((N_HEAD_A, SUBLANES, LANES), F32),
        pltpu.VMEM((SUBLANES, W_B), F32),
        pltpu.VMEM((BLK_C, KV_W_C), BF16),
        pltpu.VMEM((BLK_C, KV_W_C), BF16),
        pltpu.VMEM((SUBLANES, BLK_C), jnp.int32),
        pltpu.VMEM((N_HEAD_D, LANES, LANES), F32),
    ]
    return pl.pallas_call(
        functools.partial(_layer_kernel, layer=layer, final=final, tq=tq),
        grid=(bsz, seq // tq),
        in_specs=in_specs,
        out_specs=pl.BlockSpec((1, tq, d_model), lambda b, c: (b, c, 0)),
        out_shape=jax.ShapeDtypeStruct(x.shape, x.dtype),
        scratch_shapes=scratch,
        compiler_params=pltpu.CompilerParams(dimension_semantics=("arbitrary", "arbitrary"),
                                             vmem_limit_bytes=VMEM_LIMIT_BYTES),
        name=f"hybrid_layer_{layer}",
    )(x, pos_col, pos_row, mem_kv, *[p for _, p in params])


def _tile_rows(seq):
    for t in (256, 128):
        if seq % t == 0:
            return t
    raise ValueError("sequence length must be a multiple of 128")


def kernel(x, mem, positions, norm_g, w_in, mlstm_conv_w, mlstm_b_if, mlstm_norm_g, lru_conv_w, lru_conv_b, lru_wa, lru_ba, lru_wx, lru_bx, lru_lambda, swa_sinks, hgrn_lb, hgrn_norm_g, mem_norm_g, w_mem_kv, w_br, w_out, final_norm_g):
    depth, d_model, n_in = w_in.shape
    n_a = 5 * W_A
    assert d_model == D_MODEL and n_in == n_a + N_IF + N_REST
    tq = _tile_rows(x.shape[1])

    w_in_rest, w_in_a = _w_in_prep_call(w_in, n_a)
    b_if = jnp.pad(mlstm_b_if, ((0, 0), (0, LANES - N_IF)))[:, None, :]
    nb, bd = lru_wa.shape[1], lru_wa.shape[2]
    eye = jnp.eye(nb, dtype=lru_wa.dtype)

    def block_diag(w):
        return (eye[None, :, None, :, None] * w[:, :, :, None, :]).reshape(depth, nb * bd, nb * bd)

    wa_d, wx_d = block_diag(lru_wa), block_diag(lru_wx)
    hw = W_B // 2
    w_lru = jnp.stack([jnp.concatenate([wa_d[:, :hw, :hw], wx_d[:, :hw, :hw]], axis=-1),
                       jnp.concatenate([wa_d[:, hw:, hw:], wx_d[:, hw:, hw:]], axis=-1)], axis=1).astype(BF16)
    mem_kv = _mem_kv_call(mem, mem_norm_g, w_mem_kv.astype(BF16))
    w_br_b = w_br.astype(BF16)
    w_out_b = w_out.astype(BF16)

    pos_col = positions[:, :, None]
    pos_row = positions[:, None, :]
    row = lambda a, l: a[l][None, :]
    for l in range(depth):
        params = [
            ("norm_g", row(norm_g, l)), ("w_in", w_in_rest), ("w_in_a", w_in_a), ("conv_a", mlstm_conv_w[l]),
            ("b_if", b_if[l]),
            ("mlstm_norm_g", row(mlstm_norm_g, l)), ("lru_conv_w", lru_conv_w[l]), ("lru_conv_b", row(lru_conv_b, l)),
            ("w_lru", w_lru), ("lru_ba", row(lru_ba, l)), ("lru_bx", row(lru_bx, l)), ("lru_lambda", row(lru_lambda, l)),
            ("sinks", row(swa_sinks, l)), ("hgrn_lb", hgrn_lb), ("hgrn_norm_g", row(hgrn_norm_g, l)),
            ("w_br", w_br_b), ("w_out", w_out_b), ("final_norm_g", final_norm_g[None, :]),
        ]
        x = _layer_call(x, pos_col, pos_row, mem_kv, params, layer=l, final=(l == depth - 1), tq=tq)
    return x
```

```python
import functools
import math

import jax
import jax.numpy as jnp
from jax import lax
from jax.experimental import pallas as pl
from jax.experimental.pallas import tpu as pltpu

F32 = jnp.float32
BF16 = jnp.bfloat16

EPS = 1e-6
CONV_W = 4
N_HEAD_A = 4
W_A = 512
L_A = 128
INIT_M = -1e30
W_B = 512
LRU_C = 8.0
N_HEAD_C = 8
W_C = 512
KV_W_C = 128
BLK_C = 128
N_HEAD_D = 4
W_D = 512
DIAG_D = 8
W_E = 256
N_BRANCH = 5
BRANCH_W = (W_A, W_B, W_C, W_D, W_E)

LANES = 128
SUBLANES = 8
VMEM_LIMIT_BYTES = 58 * 1024 * 1024

D_MODEL = 1024
N_IF = 2 * N_HEAD_A

_OFF = {}
_acc = 0
for _name, _w in (("b_x", W_B), ("b_z", W_B),
                  ("c_q", W_C), ("c_k", KV_W_C), ("c_v", KV_W_C), ("c_z", W_C),
                  ("d_q", W_D), ("d_f", W_D), ("d_i", W_D), ("d_z", W_D),
                  ("e_q", W_E), ("e_z", W_E), ("gates", N_BRANCH * D_MODEL),
                  ("a_qk", 2 * W_A), ("a_v", W_A), ("a_o", W_A), ("a_z", W_A), ("a_if", LANES)):
    _OFF[_name] = _acc
    _acc += _w
N_REST = _OFF["a_qk"]
W_IN_A = _acc - N_REST


def _bf(x):
    return x.astype(BF16)


def _dot(a, b):
    return jnp.dot(a, b, preferred_element_type=F32)


def _dot_nt(a, b):
    return lax.dot_general(a, b, (((1,), (1,)), ((), ())), preferred_element_type=F32)


def _dot_tn(a, b):
    return lax.dot_general(a, b, (((0,), (0,)), ((), ())), preferred_element_type=F32)


def _silu(x):
    return x * jax.nn.sigmoid(x)


def _log_sigmoid(x):
    return jnp.minimum(x, 0.0) - jnp.log(1.0 + jnp.exp(-jnp.abs(x)))


def _softplus(x):
    return jnp.maximum(x, 0.0) + jnp.log(1.0 + jnp.exp(-jnp.abs(x)))


def _linear_scan_rows(a, b, h0):
    t, w = a.shape
    groups = t // SUBLANES
    a3 = a.reshape(groups, SUBLANES, w)
    b3 = b.reshape(groups, SUBLANES, w)
    sub = lax.broadcasted_iota(jnp.int32, a3.shape, 1)
    s = 1
    while s < SUBLANES:
        keep = sub >= s
        a_s = jnp.where(keep, pltpu.roll(a3, s, 1), 1.0)
        b_s = jnp.where(keep, pltpu.roll(b3, s, 1), 0.0)
        b3 = a3 * b_s + b3
        a3 = a3 * a_s
        s *= 2
    yield
    carry = h0
    out = []
    for g in range(groups):
        hg = b3[g] + a3[g] * carry
        out.append(hg)
        carry = hg[SUBLANES - 1:SUBLANES, :]
        if g % SUBLANES == SUBLANES - 1:
            yield
    return jnp.concatenate(out, axis=0)


def _rmsnorm(x, g):
    return x * lax.rsqrt(jnp.mean(x * x, axis=-1, keepdims=True) + EPS) * g


def _cumsum_rows(tri, x):
    hi = _bf(x)
    r1 = x - hi.astype(F32)
    mid = _bf(r1)
    lo = _bf(r1 - mid.astype(F32))
    return _dot(tri, hi) + _dot(tri, mid) + _dot(tri, lo)


def _block_row_bcast(b, blk, idx):
    n = b.shape[0] // blk
    b3 = b.reshape(n, blk, b.shape[1])
    return jnp.broadcast_to(b3[:, idx:idx + 1, :], b3.shape).reshape(b.shape)


def _causal_conv(tail, x, w_ref, t):
    groups = t // SUBLANES
    x3 = x.reshape(groups, SUBLANES, x.shape[1])
    sub = lax.broadcasted_iota(jnp.int32, x3.shape, 1)
    tail3 = tail[...][None]
    y = w_ref[CONV_W - 1:CONV_W, :] * x3
    for s in range(1, CONV_W):
        rolled = pltpu.roll(x3, s, 1)
        rolled_prev = jnp.concatenate([pltpu.roll(tail3, s, 1), rolled[:-1]], axis=0)
        y = y + w_ref[CONV_W - 1 - s:CONV_W - s, :] * jnp.where(sub >= s, rolled, rolled_prev)
    tail[...] = x3[groups - 1]
    return y.reshape(x.shape)


def _mlstm_head(q, k, v, g, icol, grow, irow, c_st, n_st, m_st, causal):
    L = q.shape[0]
    dm = jnp.where(causal, g - grow + irow, -jnp.inf)
    g_end = g[L - 1:L, :]
    w_end = g_end - g + icol
    m_loc = jnp.max(w_end, axis=0, keepdims=True)
    ke = k * jnp.exp(w_end - m_loc)
    vb = _bf(v)
    c_loc = _dot_tn(_bf(ke), vb)
    n_loc = jnp.sum(ke, axis=0, keepdims=True)

    a_inter = g + m_st
    m_j = jnp.maximum(a_inter, jnp.max(dm, axis=-1, keepdims=True))
    qb = _bf(q)
    p = jnp.exp(dm - m_j) * _dot_nt(qb, _bf(k))
    w_inter = jnp.exp(a_inter - m_j)
    num = _dot(_bf(p), vb) + w_inter * _dot(qb, _bf(c_st))
    den = jnp.sum(p, axis=-1, keepdims=True) + w_inter * jnp.sum(q * n_st, axis=-1, keepdims=True)
    h = num / jnp.maximum(jnp.abs(den), jnp.exp(-m_j))

    m_new = jnp.maximum(g_end + m_st, m_loc)
    a = jnp.exp(g_end + m_st - m_new)
    b = jnp.exp(m_loc - m_new)
    return h, a * c_st + b * c_loc, a * n_st + b * n_loc, m_new


def _hgrn_head(qs, kk, v, b, st_t, ri, ci):
    t = qs.shape[0]
    rows = ri[:, 0:1]
    o = _dot_nt(_bf(qs * jnp.exp(b)), _bf(st_t))
    b_end = b[t - 1:t, :]
    vb = _bf(v)
    st_new = st_t * jnp.exp(b_end) + _dot_tn(vb, _bf(kk * jnp.exp(b_end - b)))
    yield

    amat = None
    s = t // 2
    while s >= DIAG_D:
        sh = s.bit_length() - 1
        e = jnp.exp(-jnp.abs(b - _block_row_bcast(b, 2 * s, s - 1)))
        upper = (rows & (2 * s - 1)) >= s
        xl = _bf(jnp.where(upper, qs, kk) * e)
        al = _dot_nt(xl, xl)
        bj = ri >> sh
        wanted = (((bj - (ci >> sh)) << 2) + (bj & 1)) == 5
        amat = jnp.where(wanted, al, 0.0 if amat is None else amat)
        s //= 2
        yield
    sh = DIAG_D.bit_length() - 1
    d = b - _block_row_bcast(b, DIAG_D, 0)
    ad = _dot_nt(_bf(qs * jnp.exp(d)), _bf(kk * jnp.exp(-d)))
    diag = ((ri >> sh) == (ci >> sh)) & (ci <= ri)
    amat = jnp.where(diag, ad, amat)
    return o + _dot(_bf(amat), vb), st_new


def _pair_attention(qp, kmat, vmat, scale, bias_fn, sink_col):
    t = qp.shape[0]
    lane = lax.broadcasted_iota(jnp.int32, qp.shape, 1)
    lo = lane < (LANES // 2)
    assert math.log2(scale).is_integer()
    qs = qp * scale
    q2 = jnp.concatenate([jnp.where(lo, qs, 0.0), jnp.where(lo, 0.0, qs)], axis=0)
    s = bias_fn(_dot_nt(_bf(q2), kmat))
    m = jnp.max(s, axis=-1, keepdims=True)
    if sink_col is not None:
        m = jnp.maximum(m, sink_col)
    p = jnp.exp(s - m)
    den = jnp.sum(p, axis=-1, keepdims=True)
    if sink_col is not None:
        den = den + jnp.exp(sink_col - m)
    o2 = _dot(_bf(p), vmat) / den
    return jnp.where(lo, o2[:t], o2[t:])


def _run_with_fill(tasks, issue):
    for group in tasks:
        active = [(gen, pieces, [0]) for gen, pieces in group]
        while active:
            for task in list(active):
                gen, pieces, count = task
                try:
                    next(gen)
                except StopIteration:
                    active.remove(task)
                    continue
                if count[0] < len(pieces):
                    issue(pieces[count[0]])
                count[0] += 1


PREP_COLS_A = 384
PREP_COLS_REST = 1664


def _w_in_prep_kernel(wt_ref, *refs, mask_step, keep_rows):
    t = wt_ref[0]
    if mask_step is not None:
        row = lax.broadcasted_iota(jnp.int32, t.shape, 0)
        t = jnp.where((pl.program_id(1) != mask_step) | (row < keep_rows), t, 0.0)
    refs[-1][...] = _bf(t.T)


def _w_in_prep_call(w_in, n_a):
    depth, d_model, n_in = w_in.shape
    assert n_in == n_a + N_IF + N_REST and n_a + LANES == W_IN_A
    assert N_REST % PREP_COLS_REST == 0 and W_IN_A % PREP_COLS_A == 0
    w_t = jnp.swapaxes(w_in, 1, 2)
    params = pltpu.CompilerParams(dimension_semantics=("arbitrary", "arbitrary"), vmem_limit_bytes=VMEM_LIMIT_BYTES)

    def window_spec(cols, first_row):
        return pl.BlockSpec((pl.Element(1), pl.Element(cols), pl.Element(d_model)),
                            lambda l, g: (l, (g * (cols // SUBLANES) + first_row // SUBLANES) * SUBLANES, 0))

    rest = pl.pallas_call(
        functools.partial(_w_in_prep_kernel, mask_step=None, keep_rows=None),
        grid=(depth, N_REST // PREP_COLS_REST),
        in_specs=[window_spec(PREP_COLS_REST, n_a + N_IF)],
        out_specs=pl.BlockSpec((None, d_model, PREP_COLS_REST), lambda l, g: (l, 0, g)),
        out_shape=jax.ShapeDtypeStruct((depth, d_model, N_REST), BF16), compiler_params=params,
        name="w_in_prep_rest",
    )(w_t)
    steps_a = W_IN_A // PREP_COLS_A
    mlstm = pl.pallas_call(
        functools.partial(_w_in_prep_kernel, mask_step=steps_a - 1, keep_rows=(n_a + N_IF) % PREP_COLS_A),
        grid=(depth, steps_a),
        in_specs=[window_spec(PREP_COLS_A, 0)],
        out_specs=pl.BlockSpec((None, d_model, PREP_COLS_A), lambda l, g: (l, 0, g)),
        out_shape=jax.ShapeDtypeStruct((depth, d_model, W_IN_A), BF16), compiler_params=params,
        name="w_in_prep_a",
    )(w_t)
    return rest, mlstm


def _mem_kv_kernel(mem_ref, g_ref, w_ref, o_ref):
    o_ref[...] = _bf(_dot(_bf(_rmsnorm(mem_ref[0], g_ref[...])), w_ref[...]))


def _mem_kv_call(mem, mem_norm_g, w_kv):
    bsz, m_len, d_model = mem.shape
    depth, _, n_kv = w_kv.shape
    return pl.pallas_call(
        _mem_kv_kernel,
        grid=(depth, bsz),
        in_specs=[pl.BlockSpec((1, m_len, d_model), lambda l, b: (b, 0, 0)),
                  pl.BlockSpec((None, 1, d_model), lambda l, b: (l, 0, 0)),
                  pl.BlockSpec((None, d_model, n_kv), lambda l, b: (l, 0, 0))],
        out_specs=pl.BlockSpec((None, None, m_len, n_kv), lambda l, b: (l, b, 0, 0)),
        out_shape=jax.ShapeDtypeStruct((depth, bsz, m_len, n_kv), BF16),
        compiler_params=pltpu.CompilerParams(dimension_semantics=("arbitrary", "arbitrary")),
        name="mem_kv",
    )(mem, mem_norm_g[:, None, :], w_kv)


def _layer_kernel(x_ref, pc_ref, pr_ref, kv_ref, ng_ref, win_ref, wina_ref, cwa_ref, bif_ref, mng_ref,
                  lcw_ref, lcb_ref, wlru_ref, lba_ref, lbx_ref, lam_ref, sink_ref, hlb_ref, hng_ref,
                  wbr_ref, wout_ref, fg_ref,
                  o_ref,
                  conv_a, conv_b, m_c, m_n, m_m, lru_h, swa_k, swa_v, swa_p, h_st,
                  *, layer, final, tq):
    c = pl.program_id(1)
    d_model = x_ref.shape[-1]

    @pl.when(c == 0)
    def _init():
        conv_a[...] = jnp.zeros(conv_a.shape, F32)
        conv_b[...] = jnp.zeros(conv_b.shape, F32)
        m_c[...] = jnp.zeros(m_c.shape, F32)
        m_n[...] = jnp.zeros(m_n.shape, F32)
        m_m[...] = jnp.full(m_m.shape, INIT_M, F32)
        lru_h[...] = jnp.zeros(lru_h.shape, F32)
        swa_k[...] = jnp.zeros(swa_k.shape, BF16)
        swa_v[...] = jnp.zeros(swa_v.shape, BF16)
        swa_p[...] = jnp.zeros(swa_p.shape, jnp.int32)
        h_st[...] = jnp.zeros(h_st.shape, F32)

    x = x_ref[0]
    hb = _bf(_rmsnorm(x, ng_ref[...]))

    def w_cols(o, width):
        if o >= N_REST:
            return wina_ref[:, o - N_REST:o - N_REST + width]
        return win_ref[:, o:o + width]

    gate_piece = d_model // 2
    n_gate_pieces = d_model // gate_piece
    widths = {"a_o": W_A, "a_z": W_A, "b_x": W_B, "b_z": W_B, "c_q": W_C, "c_kv": 2 * KV_W_C, "c_z": W_C,
              "d_q": W_D, "d_f": W_D, "d_i": W_D, "d_z": W_D, "e_q": W_E, "e_z": W_E}
    offsets = dict(_OFF, c_kv=_OFF["c_k"])
    gates = [("gate", j, hf) for j in range(N_BRANCH) for hf in range(n_gate_pieces)]
    pending = ["a_o", "a_z", "b_x", "b_z", "c_q", "c_kv"] + gates + ["c_z", "d_f", "d_q", "d_i", "d_z", "e_q", "e_z"]
    issued = {}

    def issue(piece):
        if isinstance(piece, tuple):
            _, j, hf = piece
            g0 = _OFF["gates"] + j * d_model + hf * gate_piece
            issued[piece] = jax.nn.sigmoid(_dot(hb, w_cols(g0, gate_piece)))
        else:
            o = offsets[piece]
            issued[piece] = _dot(hb, w_cols(o, widths[piece]))

    def issue_pending(n):
        for _ in range(min(n, len(pending))):
            issue(pending.pop(0))

    def proj(name, width):
        if name in widths:
            assert widths[name] == width
            if name not in issued:
                pending.remove(name)
                issue(name)
            return issued[name]
        o = _OFF[name]
        return _dot(hb, w_cols(o, width))

    branch_out = {}
    hd = LANES // 2

    def mixer_a():
        qk = _silu(_causal_conv(conv_a, proj("a_qk", 2 * W_A), cwa_ref, tq))
        yield
        v_a = proj("a_v", W_A)
        gif = proj("a_if", LANES) + bif_ref[...]
        logf = _log_sigmoid(gif)
        ri_a = lax.broadcasted_iota(jnp.int32, (L_A, L_A), 0)
        ci_a = lax.broadcasted_iota(jnp.int32, (L_A, L_A), 1)
        causal_a = ci_a <= ri_a
        tri_a = jnp.where(causal_a, 1.0, 0.0).astype(BF16)
        hm_chunks = []
        for n in range(tq // L_A):
            rs = slice(n * L_A, (n + 1) * L_A)
            gif_c = gif[rs]
            g_all = _cumsum_rows(tri_a, logf[rs])
            g_t = g_all.T
            gif_t = gif_c.T
            heads = []
            for h in range(N_HEAD_A):
                ls = slice(h * LANES, (h + 1) * LANES)
                hh, c_new, n_new, m_new = _mlstm_head(
                    qk[rs, ls] * (LANES ** -0.5), qk[rs, W_A + h * LANES:W_A + (h + 1) * LANES], v_a[rs, ls],
                    g_all[:, N_HEAD_A + h:N_HEAD_A + h + 1], gif_c[:, h:h + 1],
                    g_t[N_HEAD_A + h:N_HEAD_A + h + 1, :], gif_t[h:h + 1, :],
                    m_c[h], m_n[h, 0:1, :], m_m[h, 0:1, 0:1], causal_a)
                m_c[h] = c_new
                m_n[h] = jnp.broadcast_to(n_new, m_n.shape[1:])
                m_m[h] = jnp.broadcast_to(m_new, m_m.shape[1:])
                hh = hh * lax.rsqrt(jnp.mean(hh * hh, axis=-1, keepdims=True) + EPS)
                heads.append(hh)
                yield
            hm_chunks.append(jnp.concatenate(heads, axis=1))
        hm = jnp.concatenate(hm_chunks, axis=0) if len(hm_chunks) > 1 else hm_chunks[0]
        branch_out["a"] = hm * mng_ref[...] * jax.nn.sigmoid(proj("a_o", W_A)) * _silu(proj("a_z", W_A))

    def mixer_b():
        xc = _causal_conv(conv_b, proj("b_x", W_B), lcw_ref, tq) + lcb_ref[...]
        yield
        xcb = _bf(xc)
        half = W_B // 2
        rx0 = _dot(xcb[:, :half], wlru_ref[0])
        rx1 = _dot(xcb[:, half:], wlru_ref[1])
        r = jax.nn.sigmoid(jnp.concatenate([rx0[:, :half], rx1[:, :half]], axis=1) + lba_ref[...])
        i_g = jax.nn.sigmoid(jnp.concatenate([rx0[:, half:], rx1[:, half:]], axis=1) + lbx_ref[...])
        yield
        a_t = jnp.exp(-LRU_C * r * _softplus(-lam_ref[...]))
        b_t = jnp.sqrt(1.0 - a_t * a_t) * (i_g * xc)
        yield
        h_lru = yield from _linear_scan_rows(a_t, b_t, lru_h[0:1, :])
        lru_h[...] = jnp.broadcast_to(h_lru[tq - 1:tq, :], lru_h.shape)
        branch_out["b"] = h_lru * _silu(proj("b_z", W_B))

    def mixer_c():
        q_c = proj("c_q", W_C)
        kv_c = _bf(proj("c_kv", 2 * KV_W_C))
        k_c = kv_c[:, :KV_W_C]
        v_c = kv_c[:, KV_W_C:]
        pos_c = pc_ref[0]
        pos_r = pr_ref[0]
        rows2 = lax.broadcasted_iota(jnp.int32, (2 * BLK_C, 1), 0)
        ri_c = lax.broadcasted_iota(jnp.int32, (BLK_C, 2 * BLK_C), 0)
        ci_c = lax.broadcasted_iota(jnp.int32, (BLK_C, 2 * BLK_C), 1)
        rel = ri_c + BLK_C - ci_c
        in_window = (rel >= 0) & (rel < BLK_C)
        yc_blocks = []
        for n in range(tq // BLK_C):
            rs = slice(n * BLK_C, (n + 1) * BLK_C)
            k_prev = swa_k[...] if n == 0 else k_c[(n - 1) * BLK_C:n * BLK_C]
            v_prev = swa_v[...] if n == 0 else v_c[(n - 1) * BLK_C:n * BLK_C]
            p_prev = swa_p[0:1, :] if n == 0 else pos_r[:, (n - 1) * BLK_C:n * BLK_C]
            kk = jnp.concatenate([k_prev, k_c[rs]], axis=0)
            vv = jnp.concatenate([v_prev, v_c[rs]], axis=0)
            pk = jnp.concatenate([p_prev, pos_r[:, rs]], axis=1)
            dist = jnp.abs(pos_c[rs] - pk).astype(F32)
            first_key = jnp.where((c * (tq // BLK_C) + n) == 0, BLK_C, 0)
            valid = in_window & (ci_c >= first_key)
            dist2 = jnp.concatenate([dist, dist], axis=0)
            valid2 = jnp.concatenate([valid, valid], axis=0)
            kdup = [jnp.concatenate([kk[:, j * hd:(j + 1) * hd]] * 2, axis=1) for j in range(2)]
            vdup = [jnp.concatenate([vv[:, j * hd:(j + 1) * hd]] * 2, axis=1) for j in range(2)]
            pairs = []
            for p in range(N_HEAD_C // 2):
                kvh = (2 * p) // (N_HEAD_C // 2)
                top = rows2 < BLK_C
                slope = jnp.where(top, 2.0 ** (-(2 * p + 1)), 2.0 ** (-(2 * p + 2)))
                sink = jnp.where(top, sink_ref[0, 2 * p], sink_ref[0, 2 * p + 1])

                def bias(sc, slope=slope, dist2=dist2, valid2=valid2):
                    return jnp.where(valid2, sc - slope * dist2, -jnp.inf)

                pairs.append(_pair_attention(q_c[rs, p * LANES:(p + 1) * LANES], kdup[kvh], vdup[kvh],
                                             hd ** -0.5, bias, sink))
                yield
            yc_blocks.append(jnp.concatenate(pairs, axis=1))
        last = slice(tq - BLK_C, tq)
        swa_k[...] = k_c[last]
        swa_v[...] = v_c[last]
        swa_p[...] = jnp.broadcast_to(pos_r[:, last], swa_p.shape)
        yc = jnp.concatenate(yc_blocks, axis=0) if len(yc_blocks) > 1 else yc_blocks[0]
        branch_out["c"] = yc * _silu(proj("c_z", W_C))

    def mixer_d():
        lbp = hlb_ref[...]
        e_lb = jnp.exp(lbp - jnp.max(lbp, axis=0, keepdims=True))
        p_lb = e_lb / jnp.sum(e_lb, axis=0, keepdims=True)
        lb = jnp.zeros((1, W_D), F32)
        for j in range(1, layer + 1):
            lb = lb + p_lb[j:j + 1, :]
        ls_f = jnp.log1p(-lb) + _log_sigmoid(proj("d_f", W_D))
        log_lb = jnp.log(lb)
        mx = jnp.maximum(log_lb, ls_f)
        log_f = mx + jnp.log(1.0 + jnp.exp(-jnp.abs(log_lb - ls_f)))
        k_d = 1.0 - jnp.exp(log_f)
        yield
        ri_d = lax.broadcasted_iota(jnp.int32, (tq, tq), 0)
        ci_d = lax.broadcasted_iota(jnp.int32, (tq, tq), 1)
        tri_d = jnp.where(ci_d <= ri_d, 1.0, 0.0).astype(BF16)
        b_all = _cumsum_rows(tri_d, log_f)
        qs_d = _silu(proj("d_q", W_D)) * (LANES ** -0.5)
        v_d = proj("d_i", W_D)
        yield
        hd_heads = []
        for h in range(N_HEAD_D):
            ls = slice(h * LANES, (h + 1) * LANES)
            o_h, st_new = yield from _hgrn_head(qs_d[:, ls], k_d[:, ls], v_d[:, ls], b_all[:, ls], h_st[h],
                                                ri_d, ci_d)
            h_st[h] = st_new
            hd_heads.append(o_h * lax.rsqrt(jnp.mean(o_h * o_h, axis=-1, keepdims=True) + EPS))
            yield
        branch_out["d"] = jnp.concatenate(hd_heads, axis=1) * hng_ref[...] * _silu(proj("d_z", W_D))

    def mixer_e():
        q_e = proj("e_q", W_E)
        pairs = []
        for p in range(W_E // LANES):
            ls = slice(p * LANES, (p + 1) * LANES)
            pairs.append(_pair_attention(q_e[:, ls], kv_ref[:, ls], kv_ref[:, W_E + p * LANES:W_E + (p + 1) * LANES],
                                         hd ** -0.5, lambda sc: sc, None))
            yield
        branch_out["e"] = jnp.concatenate(pairs, axis=1) * _silu(proj("e_z", W_E))

    _run_with_fill([[(mixer_a(), [2] * (N_HEAD_A * (tq // L_A)))],
                    [(mixer_b(), [2, 0, 2])],
                    [(mixer_c(), [1] * (N_HEAD_C // 2 * (tq // BLK_C)))],
                    [(mixer_d(), [])],
                    [(mixer_e(), [])]], issue_pending)

    issue_pending(len(pending))
    y_pieces = []
    for hf in range(n_gate_pieces):
        cs = slice(hf * gate_piece, (hf + 1) * gate_piece)
        y = None
        row0 = 0
        for j, (name, wj) in enumerate(zip("abcde", BRANCH_W)):
            t_j = issued[("gate", j, hf)] * _dot(_bf(branch_out[name]), wbr_ref[row0:row0 + wj, cs])
            y = t_j if y is None else y + t_j
            row0 += wj
        y_pieces.append(_bf(y))
    x_new = x + _dot(jnp.concatenate(y_pieces, axis=1), wout_ref[...])
    if final:
        x_new = _rmsnorm(x_new, fg_ref[...])
    o_ref[0] = x_new


def _const_spec(shape):
    nd = len(shape)
    return pl.BlockSpec(shape, lambda b, c: (0,) * nd, pipeline_mode=pl.Buffered(1))


_STACKED = ("w_in", "w_in_a", "w_lru", "w_br", "w_out")


def _layer_spec(shape, layer):
    nd = len(shape)
    return pl.BlockSpec((None,) + tuple(shape[1:]), lambda b, c: (layer,) + (0,) * (nd - 1),
                        pipeline_mode=pl.Buffered(1))


def _layer_call(x, pos_col, pos_row, mem_kv, params, *, layer, final, tq):
    bsz, seq, d_model = x.shape
    in_specs = [
        pl.BlockSpec((1, tq, d_model), lambda b, c: (b, c, 0)),
        pl.BlockSpec((1, tq, 1), lambda b, c: (b, c, 0)),
        pl.BlockSpec((1, 1, tq), lambda b, c: (b, 0, c)),
        pl.BlockSpec((None, None) + tuple(mem_kv.shape[2:]), lambda b, c: (layer, b, 0, 0)),
    ]
    for name, p in params:
        if name == "sinks":
            in_specs.append(pl.BlockSpec(memory_space=pltpu.SMEM))
        elif name in _STACKED:
            in_specs.append(_layer_spec(p.shape, layer))
        else:
            in_specs.append(_const_spec(p.shape))
    scratch = [
        pltpu.VMEM((SUBLANES, 2 * W_A), F32),
        pltpu.VMEM((SUBLANES, W_B), F32),
        pltpu.VMEM((N_HEAD_A, LANES, LANES), F32),
        pltpu.VMEM((N_HEAD_A, SUBLANES, LANES), F32),
        pltpu.VMEM((N_HEAD_A, SUBLANES, LANES), F32),
        pltpu.VMEM((SUBLANES, W_B), F32),
        pltpu.VMEM((BLK_C, KV_W_C), BF16),
        pltpu.VMEM((BLK_C, KV_W_C), BF16),
        pltpu.VMEM((SUBLANES, BLK_C), jnp.int32),
        pltpu.VMEM((N_HEAD_D, LANES, LANES), F32),
    ]
    return pl.pallas_call(
        functools.partial(_layer_kernel, layer=layer, final=final, tq=tq),
        grid=(bsz, seq // tq),
        in_specs=in_specs,
        out_specs=pl.BlockSpec((1, tq, d_model), lambda b, c: (b, c, 0)),
        out_shape=jax.ShapeDtypeStruct(x.shape, x.dtype),
        scratch_shapes=scratch,
        compiler_params=pltpu.CompilerParams(dimension_semantics=("arbitrary", "arbitrary"),
                                             vmem_limit_bytes=VMEM_LIMIT_BYTES),
        name=f"hybrid_layer_{layer}",
    )(x, pos_col, pos_row, mem_kv, *[p for _, p in params])


def _tile_rows(seq):
    for t in (256, 128):
        if seq % t == 0:
            return t
    raise ValueError("sequence length must be a multiple of 128")


def kernel(x, mem, positions, norm_g, w_in, mlstm_conv_w, mlstm_b_if, mlstm_norm_g, lru_conv_w, lru_conv_b, lru_wa, lru_ba, lru_wx, lru_bx, lru_lambda, swa_sinks, hgrn_lb, hgrn_norm_g, mem_norm_g, w_mem_kv, w_br, w_out, final_norm_g):
    depth, d_model, n_in = w_in.shape
    n_a = 5 * W_A
    assert d_model == D_MODEL and n_in == n_a + N_IF + N_REST
    tq = _tile_rows(x.shape[1])

    w_in_rest, w_in_a = _w_in_prep_call(w_in, n_a)
    b_if = jnp.pad(mlstm_b_if, ((0, 0), (0, LANES - N_IF)))[:, None, :]
    nb, bd = lru_wa.shape[1], lru_wa.shape[2]
    eye = jnp.eye(nb, dtype=lru_wa.dtype)

    def block_diag(w):
        return (eye[None, :, None, :, None] * w[:, :, :, None, :]).reshape(depth, nb * bd, nb * bd)

    wa_d, wx_d = block_diag(lru_wa), block_diag(lru_wx)
    hw = W_B // 2
    w_lru = jnp.stack([jnp.concatenate([wa_d[:, :hw, :hw], wx_d[:, :hw, :hw]], axis=-1),
                       jnp.concatenate([wa_d[:, hw:, hw:], wx_d[:, hw:, hw:]], axis=-1)], axis=1).astype(BF16)
    mem_kv = _mem_kv_call(mem, mem_norm_g, w_mem_kv.astype(BF16))
    w_br_b = w_br.astype(BF16)
    w_out_b = w_out.astype(BF16)

    pos_col = positions[:, :, None]
    pos_row = positions[:, None, :]
    row = lambda a, l: a[l][None, :]
    for l in range(depth):
        params = [
            ("norm_g", row(norm_g, l)), ("w_in", w_in_rest), ("w_in_a", w_in_a), ("conv_a", mlstm_conv_w[l]),
            ("b_if", b_if[l]),
            ("mlstm_norm_g", row(mlstm_norm_g, l)), ("lru_conv_w", lru_conv_w[l]), ("lru_conv_b", row(lru_conv_b, l)),
            ("w_lru", w_lru), ("lru_ba", row(lru_ba, l)), ("lru_bx", row(lru_bx, l)), ("lru_lambda", row(lru_lambda, l)),
            ("sinks", row(swa_sinks, l)), ("hgrn_lb", hgrn_lb), ("hgrn_norm_g", row(hgrn_norm_g, l)),
            ("w_br", w_br_b), ("w_out", w_out_b), ("final_norm_g", final_norm_g[None, :]),
        ]
        x = _layer_call(x, pos_col, pos_row, mem_kv, params, layer=l, final=(l == depth - 1), tq=tq)
    return x
```
